```python
import math
import jax, jax.numpy as jnp
from jax import lax
import numpy as np

D_MODEL = 2048
BATCH = 4
SEQ = 2048
DEPTH = 4
DEC_BATCH = 8
DEC_SEQ = 4
PAST_LEN = 16384
PAGE_SIZE = 128

D_MIX = D_MODEL
HEAD_DIM = 64
D_SSD = D_MIX // 2
SSD_HEADDIM = 64
H_SSD = D_SSD // SSD_HEADDIM
SSD_GROUPS = 2
SSD_HPG = H_SSD // SSD_GROUPS
SSD_STATE = 128
SSD_CONV = 4
SSD_CHUNK = 128
SSD_CONV_DIM = D_SSD + 2 * SSD_GROUPS * SSD_STATE
D_FOX = D_MIX // 4
H_FOX = D_FOX // HEAD_DIM
D_DSA = D_MIX - D_SSD - D_FOX
H_DSA = D_DSA // HEAD_DIM
H_IDX = 8
D_IDX = 64
TOPK_MAX = 256
Q_BLOCK = 128
D_FF = 5632
FFN_CONV = 3
RMS_EPS = 1e-6

IN_SPLITS = (D_SSD, SSD_CONV_DIM, H_SSD,
             D_FOX, D_FOX, D_FOX, H_FOX,
             D_DSA, D_DSA, D_DSA,
             H_IDX * D_IDX, D_IDX, H_IDX)
N_IN = D_SSD + SSD_CONV_DIM + H_SSD + 3 * D_FOX + H_FOX + 3 * D_DSA + H_IDX * D_IDX + D_IDX + H_IDX

kernel_name = "hymba_ssd_fox_dsa_convffn_step"


def rmsnorm(x, g):
    xf = x.astype(jnp.float32)
    y = xf * lax.rsqrt(jnp.mean(xf * xf, axis=-1, keepdims=True) + RMS_EPS)
    return (y * g.astype(jnp.float32)).astype(x.dtype)


def split_cols(u, sizes):
    cuts = [int(s) for s in np.cumsum(sizes)[:-1]]
    return jnp.split(u, cuts, axis=-1)


def _block(t, blk):
    return blk if t % blk == 0 else t


def causal_dwconv(x, buf, w, b):
    width = w.shape[0]
    t = x.shape[1]
    xx = jnp.concatenate([buf.astype(x.dtype), x], axis=1)
    y = b + sum(xx[:, i:i + t] * w[i] for i in range(width))
    return y, xx[:, t:]


def segsum(a):
    t = a.shape[-1]
    xx = jnp.broadcast_to(a[..., :, None], a.shape + (t,))
    strict = jnp.tril(jnp.ones((t, t), bool), -1)
    ss = jnp.cumsum(jnp.where(strict, xx, 0.0), axis=-2)
    return jnp.where(jnp.tril(jnp.ones((t, t), bool)), ss, -jnp.inf)


def ssd_scan(X, A, B, C, h0, chunk):
    b, t, g, e, p = X.shape
    n = B.shape[-1]
    nc = t // chunk
    X = X.reshape(b, nc, chunk, g, e, p)
    B = B.reshape(b, nc, chunk, g, n)
    C = C.reshape(b, nc, chunk, g, n)
    A = A.reshape(b, nc, chunk, g, e).transpose(0, 3, 4, 1, 2)
    A_cs = jnp.cumsum(A, axis=-1)
    Lmat = jnp.exp(segsum(A))
    y_diag = jnp.einsum('bclgn,bcsgn,bgecls,bcsgep->bclgep', C, B, Lmat, X)
    decay_states = jnp.exp(A_cs[..., -1:] - A_cs)
    states = jnp.einsum('bclgn,bgecl,bclgep->bcgepn', B, decay_states, X)
    states = jnp.concatenate([h0[:, None], states], axis=1)
    chunk_decay = jnp.exp(segsum(jnp.pad(A_cs[..., -1], ((0, 0), (0, 0), (0, 0), (1, 0)))))
    new_states = jnp.einsum('bgezc,bcgepn->bzgepn', chunk_decay, states)
    states_in, final = new_states[:, :-1], new_states[:, -1]
    y_off = jnp.einsum('bclgn,bcgepn,bgecl->bclgep', C, states_in, jnp.exp(A_cs))
    return (y_diag + y_off).reshape(b, t, g, e, p), final


def ssd_mixer(z, xbc, dt_raw, conv_buf, h0, conv_w, conv_b, dt_bias, a_log, d_skip, norm_g):
    b, t, _ = z.shape
    xbc, conv_new = causal_dwconv(xbc, conv_buf, conv_w, conv_b)
    xbc = jax.nn.silu(xbc)
    xs, Bm, Cm = split_cols(xbc, (D_SSD, SSD_GROUPS * SSD_STATE, SSD_GROUPS * SSD_STATE))
    xs = xs.astype(jnp.float32).reshape(b, t, SSD_GROUPS, SSD_HPG, SSD_HEADDIM)
    Bm = Bm.astype(jnp.float32).reshape(b, t, SSD_GROUPS, SSD_STATE)
    Cm = Cm.astype(jnp.float32).reshape(b, t, SSD_GROUPS, SSD_STATE)
    dt = jax.nn.softplus(dt_raw.astype(jnp.float32) + dt_bias.astype(jnp.float32))
    dt = dt.reshape(b, t, SSD_GROUPS, SSD_HPG)
    A = -jnp.exp(a_log.astype(jnp.float32)).reshape(SSD_GROUPS, SSD_HPG)
    h0 = h0.astype(jnp.float32).reshape(b, SSD_GROUPS, SSD_HPG, SSD_HEADDIM, SSD_STATE)
    y, h_new = ssd_scan(xs * dt[..., None], dt * A, Bm, Cm, h0, _block(t, SSD_CHUNK))
    y = y + d_skip.astype(jnp.float32).reshape(SSD_GROUPS, SSD_HPG)[:, :, None] * xs
    y = y.reshape(b, t, D_SSD) * jax.nn.silu(z.astype(jnp.float32))
    gs = D_SSD // SSD_GROUPS
    y = rmsnorm(y.reshape(b, t, SSD_GROUPS, gs), norm_g.reshape(SSD_GROUPS, gs)).reshape(b, t, D_SSD)
    return (y.astype(z.dtype), conv_new,
            h_new.reshape(b, H_SSD, SSD_HEADDIM, SSD_STATE).astype(z.dtype))


def fox_attend(q, k, v, cq, ck, qpos, kpos):
    b, tq, h, d = q.shape
    qb = _block(tq, Q_BLOCK)
    ckT = jnp.swapaxes(ck, 1, 2)

    def one_block(i):
        s0 = i * qb
        qs = lax.dynamic_slice_in_dim(q, s0, qb, 1)
        cs = lax.dynamic_slice_in_dim(cq, s0, qb, 1)
        ps = lax.dynamic_slice_in_dim(qpos, s0, qb, 0)
        s = jnp.einsum('bqhd,bkhd->bhqk', qs, k).astype(jnp.float32) * (d ** -0.5)
        s = s + jnp.swapaxes(cs, 1, 2)[..., None] - ckT[:, :, None, :]
        s = jnp.where(kpos[None, :] <= ps[:, None], s, -jnp.inf)
        p = jax.nn.softmax(s, axis=-1)
        return jnp.einsum('bhqk,bkhd->bqhd', p.astype(v.dtype), v)

    out = lax.map(one_block, jnp.arange(tq // qb))
    return jnp.moveaxis(out, 0, 1).reshape(b, tq, h, d)


def dsa_attend(q, qi, wi, ki, qpos, kpos, topk, gather):
    b, tq, h, d = q.shape
    qb = _block(tq, Q_BLOCK)

    def one_block(i):
        s0 = i * qb
        qs = lax.dynamic_slice_in_dim(q, s0, qb, 1)
        qis = lax.dynamic_slice_in_dim(qi, s0, qb, 1)
        wis = lax.dynamic_slice_in_dim(wi, s0, qb, 1)
        ps = lax.dynamic_slice_in_dim(qpos, s0, qb, 0)
        dots = jnp.einsum('bqhd,bkd->bqhk', qis, ki).astype(jnp.float32) * (D_IDX ** -0.5)
        score = jnp.einsum('bqh,bqhk->bqk', wis.astype(jnp.float32) * (H_IDX ** -0.5), jax.nn.relu(dots))
        score = jnp.where((kpos[None, :] <= ps[:, None])[None], score, -jnp.inf)
        _, idx = lax.top_k(score, topk)
        ks, vs = gather(idx)
        valid = kpos[idx] <= ps[None, :, None]
        logits = jnp.einsum('bqhd,bqkhd->bhqk', qs, ks).astype(jnp.float32) * (d ** -0.5)
        logits = jnp.where(valid[:, None], logits, -jnp.inf)
        p = jax.nn.softmax(logits, axis=-1)
        return jnp.einsum('bhqk,bqkhd->bqhd', p.astype(vs.dtype), vs)

    out = lax.map(one_block, jnp.arange(tq // qb))
    return jnp.moveaxis(out, 0, 1).reshape(b, tq, h, d)


def trunk_layer(l, x, c, W, past):
    b, t, _ = x.shape
    mod = (jax.nn.silu(c) @ W['w_ada'][l] + W['b_ada'][l])[:, None, :]
    sh1, sc1, g1, sh2, sc2, g2 = jnp.split(mod, 6, axis=-1)
    h = rmsnorm(x, W['attn_norm'][l]) * (1 + sc1) + sh1
    (z, xbc, dt, fq, fk, fv, ff, dq, dk, dv, iq, ik, iw) = split_cols(h @ W['w_in'][l], IN_SPLITS)

    if past is None:
        conv_buf = jnp.zeros((b, SSD_CONV - 1, SSD_CONV_DIM), x.dtype)
        h0 = jnp.zeros((b, H_SSD, SSD_HEADDIM, SSD_STATE), x.dtype)
        ffn_buf = jnp.zeros((b, FFN_CONV - 1, 2 * D_FF), x.dtype)
    else:
        conv_buf = past['state_ssd_conv'][l]
        h0 = past['state_ssd'][l]
        ffn_buf = past['state_ffn_conv'][l]

    y_ssd, ssd_conv_new, ssd_new = ssd_mixer(
        z, xbc, dt, conv_buf, h0, W['ssd_conv_w'][l], W['ssd_conv_b'][l], W['ssd_dt_bias'][l],
        W['ssd_a_log'][l], W['ssd_d'][l], W['ssd_norm'][l])

    fq = rmsnorm(fq.reshape(b, t, H_FOX, HEAD_DIM), W['fox_q_norm'][l])
    fk = rmsnorm(fk.reshape(b, t, H_FOX, HEAD_DIM), W['fox_k_norm'][l])
    fv = fv.reshape(b, t, H_FOX, HEAD_DIM)
    logf = jax.nn.log_sigmoid(ff.astype(jnp.float32) + W['fox_f_bias'][l].astype(jnp.float32))
    dq = rmsnorm(dq.reshape(b, t, H_DSA, HEAD_DIM), W['dsa_q_norm'][l])
    dk = rmsnorm(dk.reshape(b, t, H_DSA, HEAD_DIM), W['dsa_k_norm'][l])
    dv = dv.reshape(b, t, H_DSA, HEAD_DIM)
    iq = iq.reshape(b, t, H_IDX, D_IDX)
    bidx = jnp.arange(b)[:, None, None]

    if past is None:
        pos = jnp.arange(t)
        cum = jnp.cumsum(logf, axis=1)
        y_fox = fox_attend(fq, fk, fv, cum, cum, pos, pos)

        def gather(idx):
            return dk[bidx, idx], dv[bidx, idx]

        y_dsa = dsa_attend(dq, iq, iw, ik, pos, pos, min(TOPK_MAX, t // 4), gather)
    else:
        pt = past['page_table']
        L = PAST_LEN + t
        kpos = jnp.arange(L)
        qpos = PAST_LEN + jnp.arange(t)
        pk = past['cache_fox_k'][l, pt].reshape(b, PAST_LEN, H_FOX, HEAD_DIM)
        pv = past['cache_fox_v'][l, pt].reshape(b, PAST_LEN, H_FOX, HEAD_DIM)
        plf = past['cache_fox_logf'][l, pt].reshape(b, PAST_LEN, H_FOX).astype(jnp.float32)
        lf_all = jnp.concatenate([plf, logf], axis=1)
        neg_rev = lf_all - jnp.flip(jnp.cumsum(jnp.flip(lf_all, 1), axis=1), 1)
        y_fox = fox_attend(fq, jnp.concatenate([pk, fk.astype(pk.dtype)], 1),
                           jnp.concatenate([pv, fv.astype(pv.dtype)], 1),
                           neg_rev[:, PAST_LEN:], neg_rev, qpos, kpos)
        pki = past['cache_dsa_kidx'][l, pt].reshape(b, PAST_LEN, D_IDX)
        ki_all = jnp.concatenate([pki, ik.astype(pki.dtype)], axis=1)
        pool_k, pool_v = past['cache_dsa_k'], past['cache_dsa_v']

        def gather(idx):
            in_past = (idx < PAST_LEN)[..., None, None]
            pidx = jnp.minimum(idx, PAST_LEN - 1)
            phys = pt[bidx, pidx // PAGE_SIZE]
            off = pidx % PAGE_SIZE
            nidx = jnp.clip(idx - PAST_LEN, 0, t - 1)
            ks = jnp.where(in_past, pool_k[l, phys, off], dk[bidx, nidx])
            vs = jnp.where(in_past, pool_v[l, phys, off], dv[bidx, nidx])
            return ks, vs

        y_dsa = dsa_attend(dq, iq, iw, ki_all, qpos, kpos, min(TOPK_MAX, L // 4), gather)

    mixed = jnp.concatenate([y_ssd, y_fox.reshape(b, t, D_FOX).astype(y_ssd.dtype),
                             y_dsa.reshape(b, t, D_DSA).astype(y_ssd.dtype)], axis=-1)
    x = x + g1 * (mixed @ W['w_out'][l])

    h2 = rmsnorm(x, W['ffn_norm'][l]) * (1 + sc2) + sh2
    u, ffn_conv_new = causal_dwconv(h2 @ W['w_up'][l], ffn_buf, W['ffn_conv_w'][l], W['ffn_conv_b'][l])
    a, gt = jnp.split(u, 2, axis=-1)
    x = x + g2 * ((jax.nn.silu(gt) * a) @ W['w_down'][l])

    new = (fk, fv, logf.astype(x.dtype), dk, dv, ik, ssd_new, ssd_conv_new, ffn_conv_new)
    return x, new


def setup_inputs(seed: int = 0) -> dict:
    key = jax.random.key(seed)
    keys = jax.random.split(key, 48)
    counter = [0]

    def nk():
        counter[0] += 1
        return keys[counter[0] - 1]

    def nrm(shape, scale):
        return jax.random.normal(nk(), shape, jnp.float32) * scale

    n_pages = PAST_LEN // PAGE_SIZE
    n_used = DEC_BATCH * n_pages
    n_pool = n_used + n_used // 4
    page_table = jax.random.permutation(nk(), n_pool)[:n_used].reshape(DEC_BATCH, n_pages).astype(jnp.int32)

    dt0 = jnp.exp(jax.random.uniform(nk(), (DEPTH, H_SSD), jnp.float32, math.log(1e-3), math.log(1e-1)))
    return {
        'x_prompt': nrm((BATCH, SEQ, D_MODEL), 1.0),
        'x_sample': nrm((DEC_BATCH, DEC_SEQ, D_MODEL), 1.0),
        'cache_fox_k': nrm((DEPTH, n_pool, PAGE_SIZE, H_FOX, HEAD_DIM), 1.0),
        'cache_fox_v': nrm((DEPTH, n_pool, PAGE_SIZE, H_FOX, HEAD_DIM), 1.0),
        'cache_fox_logf': jax.nn.log_sigmoid(nrm((DEPTH, n_pool, PAGE_SIZE, H_FOX), 1.0) + 3.0),
        'cache_dsa_k': nrm((DEPTH, n_pool, PAGE_SIZE, H_DSA, HEAD_DIM), 1.0),
        'cache_dsa_v': nrm((DEPTH, n_pool, PAGE_SIZE, H_DSA, HEAD_DIM), 1.0),
        'cache_dsa_kidx': nrm((DEPTH, n_pool, PAGE_SIZE, D_IDX), 1.0),
        'state_ssd': nrm((DEPTH, DEC_BATCH, H_SSD, SSD_HEADDIM, SSD_STATE), 0.1),
        'state_ssd_conv': nrm((DEPTH, DEC_BATCH, SSD_CONV - 1, SSD_CONV_DIM), 1.0),
        'state_ffn_conv': nrm((DEPTH, DEC_BATCH, FFN_CONV - 1, 2 * D_FF), 1.0),
        'page_table': page_table,
        'c_prompt': nrm((BATCH, D_MODEL), 1.0),
        'c_sample': nrm((DEC_BATCH, D_MODEL), 1.0),
        'w_ada': nrm((DEPTH, D_MODEL, 6 * D_MODEL), 0.5 * D_MODEL ** -0.5),
        'b_ada': nrm((DEPTH, 6 * D_MODEL), 0.01),
        'attn_norm': 1.0 + nrm((DEPTH, D_MODEL), 0.02),
        'w_in': nrm((DEPTH, D_MODEL, N_IN), D_MODEL ** -0.5),
        'ssd_conv_w': nrm((DEPTH, SSD_CONV, SSD_CONV_DIM), SSD_CONV ** -0.5),
        'ssd_conv_b': nrm((DEPTH, SSD_CONV_DIM), 0.01),
        'ssd_dt_bias': dt0 + jnp.log(-jnp.expm1(-dt0)),
        'ssd_a_log': jnp.log(jax.random.uniform(nk(), (DEPTH, H_SSD), jnp.float32, 1.0, 16.0)),
        'ssd_d': 1.0 + nrm((DEPTH, H_SSD), 0.02),
        'ssd_norm': 1.0 + nrm((DEPTH, D_SSD), 0.02),
        'fox_q_norm': 1.0 + nrm((DEPTH, HEAD_DIM), 0.02),
        'fox_k_norm': 1.0 + nrm((DEPTH, HEAD_DIM), 0.02),
        'fox_f_bias': jax.random.uniform(nk(), (DEPTH, H_FOX), jnp.float32, 1.0, 4.0),
        'dsa_q_norm': 1.0 + nrm((DEPTH, HEAD_DIM), 0.02),
        'dsa_k_norm': 1.0 + nrm((DEPTH, HEAD_DIM), 0.02),
        'w_out': nrm((DEPTH, D_MIX, D_MODEL), D_MIX ** -0.5),
        'ffn_norm': 1.0 + nrm((DEPTH, D_MODEL), 0.02),
        'w_up': nrm((DEPTH, D_MODEL, 2 * D_FF), D_MODEL ** -0.5),
        'ffn_conv_w': nrm((DEPTH, FFN_CONV, 2 * D_FF), FFN_CONV ** -0.5),
        'ffn_conv_b': nrm((DEPTH, 2 * D_FF), 0.01),
        'w_down': nrm((DEPTH, D_FF, D_MODEL), D_FF ** -0.5),
    }


def reference(x_prompt, x_sample, cache_fox_k, cache_fox_v, cache_fox_logf, cache_dsa_k,
              cache_dsa_v, cache_dsa_kidx, state_ssd, state_ssd_conv, state_ffn_conv, page_table,
              c_prompt, c_sample, w_ada, b_ada, attn_norm, w_in, ssd_conv_w, ssd_conv_b,
              ssd_dt_bias, ssd_a_log, ssd_d, ssd_norm, fox_q_norm, fox_k_norm, fox_f_bias,
              dsa_q_norm, dsa_k_norm, w_out, ffn_norm, w_up, ffn_conv_w, ffn_conv_b, w_down):
    W = dict(w_ada=w_ada, b_ada=b_ada, attn_norm=attn_norm, w_in=w_in, ssd_conv_w=ssd_conv_w,
             ssd_conv_b=ssd_conv_b, ssd_dt_bias=ssd_dt_bias, ssd_a_log=ssd_a_log, ssd_d=ssd_d,
             ssd_norm=ssd_norm, fox_q_norm=fox_q_norm, fox_k_norm=fox_k_norm,
             fox_f_bias=fox_f_bias, dsa_q_norm=dsa_q_norm, dsa_k_norm=dsa_k_norm, w_out=w_out,
             ffn_norm=ffn_norm, w_up=w_up, ffn_conv_w=ffn_conv_w, ffn_conv_b=ffn_conv_b,
             w_down=w_down)
    past = dict(cache_fox_k=cache_fox_k, cache_fox_v=cache_fox_v, cache_fox_logf=cache_fox_logf,
                cache_dsa_k=cache_dsa_k, cache_dsa_v=cache_dsa_v, cache_dsa_kidx=cache_dsa_kidx,
                state_ssd=state_ssd, state_ssd_conv=state_ssd_conv,
                state_ffn_conv=state_ffn_conv, page_table=page_table)
    xp, xs = x_prompt, x_sample
    new_p, new_s = [], []
    for l in range(DEPTH):
        xp, sp = trunk_layer(l, xp, c_prompt, W, None)
        xs, ss = trunk_layer(l, xs, c_sample, W, past)
        new_p.append(sp)
        new_s.append(ss)
    (fox_k_p, fox_v_p, fox_logf_p, dsa_k_p, dsa_v_p, dsa_kidx_p,
     ssd_state_p, ssd_conv_p, ffn_conv_p) = [jnp.stack(a, 0) for a in zip(*new_p)]
    (fox_k_s, fox_v_s, fox_logf_s, dsa_k_s, dsa_v_s, dsa_kidx_s,
     ssd_state_s, ssd_conv_s, ffn_conv_s) = [jnp.stack(a, 0) for a in zip(*new_s)]
    return (xp, xs,
            fox_k_p, fox_v_p, fox_logf_p, dsa_k_p, dsa_v_p, dsa_kidx_p,
            ssd_state_p, ssd_conv_p, ffn_conv_p,
            fox_k_s, fox_v_s, fox_logf_s, dsa_k_s, dsa_v_s, dsa_kidx_s,
            ssd_state_s, ssd_conv_s, ffn_conv_s)
```

```python
import functools

import jax
import jax.numpy as jnp
from jax import lax
from jax.experimental import pallas as pl
from jax.experimental.pallas import tpu as pltpu

F32 = jnp.float32
BF16 = jnp.bfloat16
I32 = jnp.int32

RMS_EPS = 1e-6
NEG = -1e30
HEAD_DIM = 64
LANES = 128
D_SSD = 1024
SSD_STATE = 128
SSD_NHEADS = 16
SSD_CONV = 4
SSD_CHUNK = 128
D_ATT = 512
N_HEADS = 8
FFN_CONV = 3
TOPK_MAX = 256
VMEM_LIMIT = 56 * 1024 * 1024

BLK = 512
BLK_Z, BLK_X, BLK_FQ, BLK_FK, BLK_FV, BLK_DQ, BLK_DK, BLK_DV, BLK_IQ, BLK_LAST = 0, 2, 5, 6, 7, 8, 9, 10, 11, 12
N_PACK = 13 * BLK
SM_DT, SM_FF, SM_IW = 0, 16, 24


def _cp(*sem):
    return pltpu.CompilerParams(dimension_semantics=sem, vmem_limit_bytes=VMEM_LIMIT)


def _dot(a, b):
    return jnp.dot(a, b, preferred_element_type=F32)


def _dot_nt(a, b):
    return lax.dot_general(a, b, (((1,), (1,)), ((), ())), preferred_element_type=F32)


def _split3(x):
    hi = x.astype(BF16)
    r = x - hi.astype(F32)
    mid = r.astype(BF16)
    lo = (r - mid.astype(F32)).astype(BF16)
    return hi, mid, lo


def _dot3_l(m01, x):
    hi, mid, lo = _split3(x)
    return _dot(m01, hi) + _dot(m01, mid) + _dot(m01, lo)


def _dot3_r(x, m01):
    hi, mid, lo = _split3(x)
    return _dot(hi, m01) + _dot(mid, m01) + _dot(lo, m01)


def _softplus(x):
    return jnp.maximum(x, 0.0) + jnp.log1p(jnp.exp(-jnp.abs(x)))


def _silu(x):
    return x * jax.nn.sigmoid(x)


def _iota(shape, dim):
    return lax.broadcasted_iota(I32, shape, dim)


def _tri(n, fn):
    r = _iota((n, n), 0)
    c = _iota((n, n), 1)
    return jnp.where(fn(r, c), 1.0, 0.0).astype(BF16)


def _ada_kernel(c_ref, w_ref, b_ref, o_ref):
    s = _silu(c_ref[...]).astype(BF16)
    o_ref[0] = _dot(s, w_ref[0].astype(BF16)) + b_ref[0]


def _ada_mod(c_all, w_ada, b_ada):
    nl, d, n = w_ada.shape
    rows = c_all.shape[0]
    tn = 1024
    return pl.pallas_call(
        _ada_kernel, grid=(nl, n // tn),
        in_specs=[pl.BlockSpec((rows, d), lambda l, j: (0, 0)),
                  pl.BlockSpec((1, d, tn), lambda l, j: (l, 0, j)),
                  pl.BlockSpec((1, 1, tn), lambda l, j: (l, 0, j))],
        out_specs=pl.BlockSpec((1, rows, tn), lambda l, j: (l, 0, j)),
        out_shape=jax.ShapeDtypeStruct((nl, rows, n), F32),
        compiler_params=_cp("arbitrary", "arbitrary"), name="ada_mod",
    )(c_all, w_ada, b_ada.reshape(nl, 1, n))


def _normmod_kernel(x_ref, g_ref, sc_ref, sh_ref, o_ref):
    x = x_ref[0]
    ms = jnp.mean(x * x, axis=-1, keepdims=True)
    y = x * lax.rsqrt(ms + RMS_EPS) * g_ref[...]
    o_ref[0] = (y * (1.0 + sc_ref[0]) + sh_ref[0]).astype(BF16)


def _norm_mod(x, g, sc, sh):
    b, t, d = x.shape
    tm = min(t, 512)
    return pl.pallas_call(
        _normmod_kernel, grid=(b, t // tm),
        in_specs=[pl.BlockSpec((1, tm, d), lambda i, m: (i, m, 0)),
                  pl.BlockSpec((1, d), lambda i, m: (0, 0)),
                  pl.BlockSpec((1, 1, d), lambda i, m: (i, 0, 0)),
                  pl.BlockSpec((1, 1, d), lambda i, m: (i, 0, 0))],
        out_specs=pl.BlockSpec((1, tm, d), lambda i, m: (i, m, 0)),
        out_shape=jax.ShapeDtypeStruct((b, t, d), BF16),
        compiler_params=_cp("arbitrary", "arbitrary"), name="norm_mod",
    )(x, g.reshape(1, d), sc, sh)


def _mm_kernel(x_ref, w_ref, o_ref):
    o_ref[0] = _dot(x_ref[0], w_ref[...])


def _mm_res_kernel(x_ref, w_ref, r_ref, g_ref, o_ref):
    o_ref[0] = r_ref[0] + g_ref[0] * _dot(x_ref[0], w_ref[...])


def _matmul(x, w, tm, tn, res=None, gate=None):
    g, t, k = x.shape
    n = w.shape[1]
    tm = min(tm, t)
    grid = (n // tn, g, t // tm)
    x_spec = pl.BlockSpec((1, tm, k), lambda j, i, m: (i, m, 0))
    w_spec = pl.BlockSpec((k, tn), lambda j, i, m: (0, j))
    o_spec = pl.BlockSpec((1, tm, tn), lambda j, i, m: (i, m, j))
    out_shape = jax.ShapeDtypeStruct((g, t, n), F32)
    cp = _cp("arbitrary", "arbitrary", "arbitrary")
    if res is None:
        return pl.pallas_call(_mm_kernel, grid=grid, in_specs=[x_spec, w_spec], out_specs=o_spec,
                              out_shape=out_shape, compiler_params=cp, name="proj")(x, w)
    if gate.shape[1] == 1:
        g_spec = pl.BlockSpec((1, 1, tn), lambda j, i, m: (i, 0, j))
    else:
        g_spec = pl.BlockSpec((1, tm, tn), lambda j, i, m: (i, m, j))
    return pl.pallas_call(_mm_res_kernel, grid=grid, in_specs=[x_spec, w_spec, o_spec, g_spec],
                          out_specs=o_spec, out_shape=out_shape, compiler_params=cp,
                          name="proj_res")(x, w, res, gate)


def _conv3(u, prev1, prev2, cw_ref, cb_ref):
    return cb_ref[...] + prev2 * cw_ref[0:1, :] + prev1 * cw_ref[1:2, :] + u * cw_ref[2:3, :]


def _shift_with_carry(u, carry):
    r1 = pltpu.roll(u, 1, 0)
    r2 = pltpu.roll(u, 2, 0)
    c1 = pltpu.roll(carry, 1, 0)
    c2 = pltpu.roll(carry, 2, 0)
    row = _iota((8, u.shape[1]), 0)
    p1 = jnp.concatenate([jnp.where(row < 1, c1, r1[0:8]), r1[8:]], axis=0)
    p2 = jnp.concatenate([jnp.where(row < 2, c2, r2[0:8]), r2[8:]], axis=0)
    return p1, p2


def _ffn_up_prompt_kernel(x_ref, wa_ref, wg_ref, cwa_ref, cwg_ref, cba_ref, cbg_ref,
                          act_ref, ta_ref, tg_ref, ca_scr, cg_scr, *, tiles_per_seq):
    m = pl.program_id(1)
    tm = x_ref.shape[0]

    @pl.when(m % tiles_per_seq == 0)
    def _():
        ca_scr[...] = jnp.zeros_like(ca_scr)
        cg_scr[...] = jnp.zeros_like(cg_scr)

    x = x_ref[...]
    ua = _dot(x, wa_ref[...])
    ug = _dot(x, wg_ref[...])
    pa1, pa2 = _shift_with_carry(ua, ca_scr[...])
    pg1, pg2 = _shift_with_carry(ug, cg_scr[...])
    a = _conv3(ua, pa1, pa2, cwa_ref, cba_ref)
    gt = _conv3(ug, pg1, pg2, cwg_ref, cbg_ref)
    act_ref[...] = (_silu(gt) * a).astype(BF16)
    ca_scr[...] = ua[tm - 8:tm]
    cg_scr[...] = ug[tm - 8:tm]
    ta_ref[0] = ua[tm - 8:tm]
    tg_ref[0] = ug[tm - 8:tm]


def _ffn_up_prompt(h, w_up, cw, cb, t_seq):
    m, k = h.shape
    f = w_up.shape[1] // 2
    tm, tn = 512, 512
    nj = f // tn
    nb = m // t_seq
    tps = t_seq // tm
    cw_spec_a = pl.BlockSpec((FFN_CONV, tn), lambda j, i: (0, j))
    cw_spec_g = pl.BlockSpec((FFN_CONV, tn), lambda j, i: (0, j + nj))
    cb_spec_a = pl.BlockSpec((1, tn), lambda j, i: (0, j))
    cb_spec_g = pl.BlockSpec((1, tn), lambda j, i: (0, j + nj))
    tail_spec = pl.BlockSpec((1, 8, tn), lambda j, i: (i // tps, 0, j))
    return pl.pallas_call(
        functools.partial(_ffn_up_prompt_kernel, tiles_per_seq=tps),
        grid=(nj, m // tm),
        in_specs=[pl.BlockSpec((tm, k), lambda j, i: (i, 0)),
                  pl.BlockSpec((k, tn), lambda j, i: (0, j)),
                  pl.BlockSpec((k, tn), lambda j, i: (0, j + nj)),
                  cw_spec_a, cw_spec_g, cb_spec_a, cb_spec_g],
        out_specs=[pl.BlockSpec((tm, tn), lambda j, i: (i, j)), tail_spec, tail_spec],
        out_shape=[jax.ShapeDtypeStruct((m, f), BF16),
                   jax.ShapeDtypeStruct((nb, 8, f), F32),
                   jax.ShapeDtypeStruct((nb, 8, f), F32)],
        scratch_shapes=[pltpu.VMEM((8, tn), F32), pltpu.VMEM((8, tn), F32)],
        compiler_params=_cp("arbitrary", "arbitrary"), name="ffn_up_prompt",
    )(h, w_up, w_up, cw, cw, cb, cb)


def _ffn_up_sample_kernel(x_ref, wa_ref, wg_ref, cwa_ref, cwg_ref, cba_ref, cbg_ref,
                          p1a_ref, p2a_ref, p1g_ref, p2g_ref, act_ref, ra_ref, rg_ref, *, t_seq):
    x = x_ref[...]
    ua = _dot(x, wa_ref[...])
    ug = _dot(x, wg_ref[...])
    t = _iota(ua.shape, 0) % t_seq

    def prevs(u, p1_ref, p2_ref):
        p1 = jnp.where(t >= 1, pltpu.roll(u, 1, 0), 0.0) + p1_ref[...]
        p2 = jnp.where(t >= 2, pltpu.roll(u, 2, 0), 0.0) + p2_ref[...]
        return p1, p2

    pa1, pa2 = prevs(ua, p1a_ref, p2a_ref)
    pg1, pg2 = prevs(ug, p1g_ref, p2g_ref)
    a = _conv3(ua, pa1, pa2, cwa_ref, cba_ref)
    gt = _conv3(ug, pg1, pg2, cwg_ref, cbg_ref)
    act_ref[...] = (_silu(gt) * a).astype(BF16)
    ra_ref[...] = ua
    rg_ref[...] = ug


def _ffn_up_sample(h, w_up, cw, cb, p1, p2, t_seq):
    m, k = h.shape
    f = w_up.shape[1] // 2
    tn = 512
    nj = f // tn
    a_spec = pl.BlockSpec((m, tn), lambda j: (0, j))
    g_spec = pl.BlockSpec((m, tn), lambda j: (0, j + nj))
    return pl.pallas_call(
        functools.partial(_ffn_up_sample_kernel, t_seq=t_seq),
        grid=(nj,),
        in_specs=[pl.BlockSpec((m, k), lambda j: (0, 0)),
                  pl.BlockSpec((k, tn), lambda j: (0, j)),
                  pl.BlockSpec((k, tn), lambda j: (0, j + nj)),
                  pl.BlockSpec((FFN_CONV, tn), lambda j: (0, j)),
                  pl.BlockSpec((FFN_CONV, tn), lambda j: (0, j + nj)),
                  pl.BlockSpec((1, tn), lambda j: (0, j)),
                  pl.BlockSpec((1, tn), lambda j: (0, j + nj)),
                  a_spec, a_spec, g_spec, g_spec],
        out_specs=[a_spec, a_spec, a_spec],
        out_shape=[jax.ShapeDtypeStruct((m, f), BF16),
                   jax.ShapeDtypeStruct((m, f), F32),
                   jax.ShapeDtypeStruct((m, f), F32)],
        compiler_params=_cp("arbitrary"), name="ffn_up_sample",
    )(h, w_up, w_up, cw, cw, cb, cb, p1, p2, p1, p2)


def _ssd_kernel(*refs, valid, has_init, nchunks):
    if has_init:
        (z0, z1, x0, x1, x2, sm_ref, cw_ref, cb_ref, dtb_row, dtb_col, a_row, a_col, d_ref, ng_ref,
         prev_ref, h0_ref, y_ref, st_ref, xx_scr, st_scr) = refs
    else:
        (z0, z1, x0, x1, x2, sm_ref, cw_ref, cb_ref, dtb_row, dtb_col, a_row, a_col, d_ref, ng_ref,
         y_ref, st_ref, xx_scr, st_scr) = refs
    c = pl.program_id(1)
    L = SSD_CHUNK
    npair = SSD_NHEADS // 2

    @pl.when(c == 0)
    def _():
        if has_init:
            xx_scr[0:8, :] = prev_ref[0]
            for j in range(npair):
                st_scr[j] = h0_ref[0, j * L:(j + 1) * L, :].T
        else:
            xx_scr[0:8, :] = jnp.zeros((8, xx_scr.shape[1]), F32)
            st_scr[...] = jnp.zeros_like(st_scr)

    xx_scr[8:8 + L, 0:BLK] = x0[0]
    xx_scr[8:8 + L, BLK:2 * BLK] = x1[0]
    xx_scr[8:8 + L, 2 * BLK:3 * BLK] = x2[0]
    acc = cb_ref[...] + xx_scr[5:5 + L, :] * cw_ref[0:1, :]
    for i in range(1, SSD_CONV):
        acc = acc + xx_scr[5 + i:5 + i + L, :] * cw_ref[i:i + 1, :]
    xx_scr[0:8, :] = xx_scr[L:L + 8, :]
    xbc = _silu(acc)
    xs = xbc[:, 0:D_SSD]
    bm = xbc[:, D_SSD:D_SSD + 2 * SSD_STATE]
    cm = xbc[:, D_SSD + 2 * SSD_STATE:D_SSD + 4 * SSD_STATE]

    small = sm_ref[0]
    dt_c = _softplus(small + dtb_row[...])
    small_t = small.T
    dt_r = _softplus(small_t + dtb_col[...])
    if valid < L:
        dt_c = jnp.where(_iota((L, LANES), 0) < valid, dt_c, 0.0)
        dt_r = jnp.where(_iota((LANES, L), 1) < valid, dt_r, 0.0)
    a_c = dt_c * a_row[...]
    a_r = dt_r * a_col[...]
    tril = _tri(L, lambda r, cc: r >= cc)
    triu = _tri(L, lambda r, cc: r <= cc)
    ones = jnp.ones((L, L), BF16)
    cs_c = _dot3_l(tril, a_c)
    cs_r = _dot3_r(a_r, triu)
    tot_c = _dot3_l(ones, a_c)
    e_cs = jnp.exp(cs_c)
    e_dec = jnp.exp(tot_c - cs_c)
    e_tot = jnp.exp(tot_c)

    lane = _iota((L, LANES), 1)
    first = lane < HEAD_DIM
    causal = _iota((L, L), 0) >= _iota((L, L), 1)

    def bcast(mat, h):
        return jnp.broadcast_to(mat[:, h:h + 1], (L, LANES))

    def pair(mat, h):
        return jnp.where(first, bcast(mat, h), bcast(mat, h + 1))

    ys = []
    for g in range(2):
        b_g = bm[:, g * SSD_STATE:(g + 1) * SSD_STATE]
        c_g = cm[:, g * SSD_STATE:(g + 1) * SSD_STATE].astype(BF16)
        cb_g = _dot_nt(c_g, b_g.astype(BF16))
        bt_g = b_g.T.astype(BF16)
        for jj in range(npair // 2):
            j = g * (npair // 2) + jj
            ha = 2 * j
            xs_p = xs[:, j * LANES:(j + 1) * LANES]
            x_p = xs_p * pair(dt_c, ha)
            x_pb = x_p.astype(BF16)
            yd = []
            for h in (ha, ha + 1):
                diff = jnp.minimum(bcast(cs_c, h) - cs_r[h:h + 1, :], 0.0)
                lm = jnp.where(causal, jnp.exp(diff), 0.0)
                yd.append(_dot((cb_g * lm).astype(BF16), x_pb))
            y_diag = jnp.where(first, yd[0], yd[1])
            st_old = st_scr[j]
            y_off = _dot(c_g, st_old.astype(BF16)) * pair(e_cs, ha)
            ys.append(y_diag + y_off + d_ref[:, j * LANES:(j + 1) * LANES] * xs_p)
            xd = (x_p * pair(e_dec, ha)).astype(BF16)
            st_scr[j] = pair(e_tot, ha) * st_old + _dot(bt_g, xd)
    y = jnp.concatenate(ys, axis=1)

    zg = jnp.concatenate([z0[0], z1[0]], axis=1)
    y = y * _silu(zg)
    half = D_SSD // 2
    outs = []
    for g in range(2):
        yg = y[:, g * half:(g + 1) * half]
        ms = jnp.mean(yg * yg, axis=-1, keepdims=True)
        outs.append(yg * lax.rsqrt(ms + RMS_EPS) * ng_ref[:, g * half:(g + 1) * half])
    y_ref[0] = jnp.concatenate(outs, axis=1).astype(BF16)

    @pl.when(c == nchunks - 1)
    def _():
        for j in range(npair):
            st_ref[0, j * L:(j + 1) * L, :] = st_scr[j].T


def _ssd(u, wl, valid, prev=None, h0=None):
    b, t, _ = u.shape
    L = SSD_CHUNK
    nch = t // L
    has_init = prev is not None

    def ublk(k):
        return pl.BlockSpec((1, L, BLK), lambda i, c: (i, c, k))

    def full(a):
        return pl.BlockSpec(a.shape, lambda i, c: (0,) * a.ndim)

    consts = [wl['ssd_conv_w'], wl['ssd_conv_b'], wl['dtb_row'], wl['dtb_col'], wl['a_row'], wl['a_col'],
              wl['d_exp'], wl['ssd_norm']]
    in_specs = [ublk(BLK_Z), ublk(BLK_Z + 1), ublk(BLK_X), ublk(BLK_X + 1), ublk(BLK_X + 2),
                pl.BlockSpec((1, L, LANES), lambda i, c: (i, c, BLK_LAST * (BLK // LANES) + 1))]
    in_specs += [full(a) for a in consts]
    args = [u, u, u, u, u, u] + consts
    if has_init:
        in_specs += [pl.BlockSpec((1, 8, 3 * BLK), lambda i, c: (i, 0, 0)),
                     pl.BlockSpec((1, SSD_NHEADS * HEAD_DIM, SSD_STATE), lambda i, c: (i, 0, 0))]
        args += [prev, h0]
    return pl.pallas_call(
        functools.partial(_ssd_kernel, valid=valid, has_init=has_init, nchunks=nch),
        grid=(b, nch), in_specs=in_specs,
        out_specs=[pl.BlockSpec((1, L, D_SSD), lambda i, c: (i, c, 0)),
                   pl.BlockSpec((1, SSD_NHEADS * HEAD_DIM, SSD_STATE), lambda i, c: (i, 0, 0))],
        out_shape=[jax.ShapeDtypeStruct((b, t, D_SSD), BF16),
                   jax.ShapeDtypeStruct((b, SSD_NHEADS * HEAD_DIM, SSD_STATE), F32)],
        scratch_shapes=[pltpu.VMEM((L + 8, 3 * BLK), F32),
                        pltpu.VMEM((SSD_NHEADS // 2, SSD_STATE, LANES), F32)],
        compiler_params=_cp("arbitrary", "arbitrary"), name="ssd",
    )(*args)


def _prep_kernel(*refs, do_cum):
    (fq, fk, fv, dq, dk, dv, last, gm, gfq, gfk, gdq, gdk, fb) = refs[:13]
    if do_cum:
        (fqn, fkn, fkb, fvb, dqn, dkn, dkb, dvb, kid, logf_o, cumc, cumr, carry) = refs[13:]
    else:
        (fqn, fkn, fkb, fvb, dqn, dkn, dkb, dvb, kid, logf_o) = refs[13:]
    m = pl.program_id(1)
    gmat = gm[...]

    def headnorm(x, g_ref):
        sq = x * x
        hi = sq.astype(BF16)
        lo = (sq - hi.astype(F32)).astype(BF16)
        ms = _dot(hi, gmat) + _dot(lo, gmat)
        return x * lax.rsqrt(ms + RMS_EPS) * g_ref[...]

    fqn[0] = headnorm(fq[0], gfq).astype(BF16)
    k1 = headnorm(fk[0], gfk)
    fkn[0] = k1
    fkb[0] = k1.astype(BF16)
    fvb[0] = fv[0].astype(BF16)
    dqn[0] = headnorm(dq[0], gdq).astype(BF16)
    k2 = headnorm(dk[0], gdk)
    dkn[0] = k2
    dkb[0] = k2.astype(BF16)
    dvb[0] = dv[0].astype(BF16)
    lst = last[0]
    kid[0] = lst[:, 0:LANES].astype(BF16)
    small = lst[:, LANES:2 * LANES]
    logf = -_softplus(-(small + fb[...]))
    logf_o[0] = logf
    if do_cum:
        tm = small.shape[0]

        @pl.when(m == 0)
        def _():
            carry[...] = jnp.zeros_like(carry)

        lane = _iota(small.shape, 1)
        lf = jnp.where((lane >= SM_FF) & (lane < SM_FF + N_HEADS), logf, 0.0)
        cum = _dot3_l(_tri(tm, lambda r, c: r >= c), lf) + carry[0:1, :]
        carry[...] = jnp.broadcast_to(cum[tm - 1:tm, :], carry.shape)
        cumc[0] = cum
        cumr[0, 0] = cum.T[SM_FF:SM_FF + N_HEADS, :]


def _prep(u, wl, tm, do_cum):
    g, t, _ = u.shape
    tm = min(tm, t)
    nt = t // tm

    def ublk(k):
        return pl.BlockSpec((1, tm, BLK), lambda i, m: (i, m, k))

    def full(a):
        return pl.BlockSpec(a.shape, lambda i, m: (0,) * a.ndim)

    consts = [wl['gmat'], wl['fox_q_norm'], wl['fox_k_norm'], wl['dsa_q_norm'], wl['dsa_k_norm'], wl['fb_row']]
    in_specs = [ublk(BLK_FQ), ublk(BLK_FK), ublk(BLK_FV), ublk(BLK_DQ), ublk(BLK_DK), ublk(BLK_DV),
                ublk(BLK_LAST)] + [full(a) for a in consts]
    o512 = pl.BlockSpec((1, tm, BLK), lambda i, m: (i, m, 0))
    o128 = pl.BlockSpec((1, tm, LANES), lambda i, m: (i, m, 0))

    def s512(dt):
        return jax.ShapeDtypeStruct((g, t, BLK), dt)

    out_specs = [o512] * 8 + [o128, o128]
    out_shape = [s512(BF16), s512(F32), s512(BF16), s512(BF16), s512(BF16), s512(F32), s512(BF16), s512(BF16),
                 jax.ShapeDtypeStruct((g, t, LANES), BF16), jax.ShapeDtypeStruct((g, t, LANES), F32)]
    scratch = []
    if do_cum:
        out_specs += [o128, pl.BlockSpec((1, 1, N_HEADS, tm), lambda i, m: (i, m, 0, 0))]
        out_shape += [jax.ShapeDtypeStruct((g, t, LANES), F32),
                      jax.ShapeDtypeStruct((g, nt, N_HEADS, tm), F32)]
        scratch = [pltpu.VMEM((8, LANES), F32)]
    return pl.pallas_call(
        functools.partial(_prep_kernel, do_cum=do_cum), grid=(g, nt),
        in_specs=in_specs, out_specs=out_specs, out_shape=out_shape, scratch_shapes=scratch,
        compiler_params=_cp("arbitrary", "arbitrary"), name="qk_prep",
    )(*([u] * 7 + consts))


def _flash_pair(q_pair, k_of, v_of, bias_of, nkb, tq):
    lane = _iota((1, LANES), 1)
    first = lane < HEAD_DIM
    zero = jnp.zeros_like(q_pair)
    scale = jnp.asarray(HEAD_DIM ** -0.5, BF16)
    qa = jnp.where(first, q_pair, zero) * scale
    qb = jnp.where(first, zero, q_pair) * scale

    def body(kb, carry):
        ma, la, mb, lb, acc = carry
        ks = k_of(kb)
        vs = v_of(kb)

        def half(qh, m_old, l_old, which):
            s = _dot_nt(qh, ks) + bias_of(kb, which)
            m_new = jnp.maximum(m_old, jnp.max(s, axis=1, keepdims=True))
            p = jnp.exp(s - m_new)
            alpha = jnp.exp(m_old - m_new)
            l_new = alpha * l_old + jnp.sum(p, axis=1, keepdims=True)
            return m_new, l_new, alpha, _dot(p.astype(BF16), vs)

        ma, la, aa, oa = half(qa, ma, la, 0)
        mb, lb, ab, ob = half(qb, mb, lb, 1)
        acc = jnp.where(first, aa, ab) * acc + jnp.where(first, oa, ob)
        return ma, la, mb, lb, acc

    m0 = jnp.full((tq, 1), NEG, F32)
    l0 = jnp.zeros((tq, 1), F32)
    ma, la, mb, lb, acc = lax.fori_loop(0, nkb, body, (m0, l0, m0, l0, jnp.zeros((tq, LANES), F32)))
    return acc * jnp.where(first, 1.0 / la, 1.0 / lb)


def _fox_prompt_kernel(q_ref, k_ref, v_ref, cc_ref, cr_ref, o_ref, *, tq):
    qi = pl.program_id(1)
    nkb = qi + 1
    qpos = qi * tq + _iota((tq, 1), 0)
    outs = []
    for j in range(D_ATT // LANES):
        sl = slice(j * LANES, (j + 1) * LANES)

        def k_of(kb, sl=sl):
            return k_ref[0, pl.ds(pl.multiple_of(kb * tq, tq), tq), sl]

        def v_of(kb, sl=sl):
            return v_ref[0, pl.ds(pl.multiple_of(kb * tq, tq), tq), sl]

        def bias_of(kb, which, j=j):
            h = 2 * j + which
            cq = cc_ref[0, :, SM_FF + h:SM_FF + h + 1]
            ck = cr_ref[0, kb][h:h + 1, :]
            kpos = kb * tq + _iota((1, tq), 1)
            return jnp.where(kpos <= qpos, cq - ck, NEG)

        outs.append(_flash_pair(q_ref[0, :, sl], k_of, v_of, bias_of, nkb, tq))
    o_ref[0] = jnp.concatenate(outs, axis=1).astype(BF16)


def _fox_prompt(q, k, v, cumc, cumr, tq):
    b, t, _ = q.shape
    nt = t // tq
    return pl.pallas_call(
        functools.partial(_fox_prompt_kernel, tq=tq), grid=(b, nt),
        in_specs=[pl.BlockSpec((1, tq, D_ATT), lambda i, m: (i, m, 0)),
                  pl.BlockSpec((1, t, D_ATT), lambda i, m: (i, 0, 0)),
                  pl.BlockSpec((1, t, D_ATT), lambda i, m: (i, 0, 0)),
                  pl.BlockSpec((1, tq, LANES), lambda i, m: (i, m, 0)),
                  pl.BlockSpec((1, nt, N_HEADS, tq), lambda i, m: (i, 0, 0, 0))],
        out_specs=pl.BlockSpec((1, tq, D_ATT), lambda i, m: (i, m, 0)),
        out_shape=jax.ShapeDtypeStruct((b, t, D_ATT), BF16),
        compiler_params=_cp("arbitrary", "arbitrary"), name="fox_prompt",
    )(q, k, v, cumc, cumr)


def _sort_key(score):
    bits = lax.bitcast_convert_type(score + 0.0, I32)
    return jnp.where(bits < 0, bits ^ jnp.int32(0x7FFFFFFF), bits)


def _kth_largest(count_ge, rows, k):
    imin = jnp.int32(-2 ** 31)

    def body(i, t):
        cand = t + lax.shift_left(jnp.int32(1), jnp.int32(31) - i)
        return jnp.where(count_ge(cand) >= k, cand, t)

    return lax.fori_loop(0, 32, body, jnp.full((rows, 1), imin, I32))


def _dsa_prompt_kernel(iq_ref, sm_ref, ki_ref, q_ref, k_ref, v_ref, o_ref, key_scr, sel_scr, *, tq, topk):
    qi = pl.program_id(1)
    nkb = qi + 1
    nk = key_scr.shape[0]
    qpos = qi * tq + _iota((tq, 1), 0)
    lane = _iota((1, LANES), 1)
    first = lane < HEAD_DIM

    wsc = sm_ref[0] * (N_HEADS ** -0.5)
    scale = jnp.asarray(HEAD_DIM ** -0.5, BF16)
    iqs = []
    for j in range(D_ATT // LANES):
        qp = iq_ref[0, :, j * LANES:(j + 1) * LANES].astype(BF16)
        zero = jnp.zeros_like(qp)
        iqs.append(jnp.where(first, qp, zero) * scale)
        iqs.append(jnp.where(first, zero, qp) * scale)
    minkey = _sort_key(jnp.full((tq, tq), -jnp.inf, F32))
    for kb in range(nk):
        @pl.when(kb >= nkb)
        def _(kb=kb):
            key_scr[kb] = minkey

    def score_body(kb, _):
        ks = ki_ref[0, pl.ds(pl.multiple_of(kb * tq, tq), tq), :]
        acc = jnp.zeros((tq, tq), F32)
        for h in range(N_HEADS):
            d = jnp.maximum(_dot_nt(iqs[h], ks), 0.0)
            acc = acc + wsc[:, SM_IW + h:SM_IW + h + 1] * d
        kpos = kb * tq + _iota((1, tq), 1)
        key_scr[kb] = _sort_key(jnp.where(kpos <= qpos, acc, -jnp.inf))
        return 0

    lax.fori_loop(0, nkb, score_body, 0)

    def count_ge(cand):
        part = jnp.zeros((tq, LANES), F32)
        for kb in range(nk):
            blk = key_scr[kb]
            for c in range(tq // LANES):
                part = part + jnp.where(blk[:, c * LANES:(c + 1) * LANES] >= cand, 1.0, 0.0)
        return jnp.sum(part, axis=1, keepdims=True)

    kf = float(topk)
    thr = _kth_largest(count_ge, tq, kf)
    n_gt = jnp.zeros((tq, LANES), F32)
    for kb in range(nk):
        blk = key_scr[kb]
        for c in range(tq // LANES):
            n_gt = n_gt + jnp.where(blk[:, c * LANES:(c + 1) * LANES] > thr, 1.0, 0.0)
    need = kf - jnp.sum(n_gt, axis=1, keepdims=True)
    strict = _tri(tq, lambda r, c: r < c)
    run = jnp.zeros((tq, 1), F32)
    for kb in range(nk):
        blk = key_scr[kb]
        eq = blk == thr
        eqf = jnp.where(eq, 1.0, 0.0)
        before = _dot(eqf.astype(BF16), strict) + run
        run = run + jnp.sum(eqf, axis=1, keepdims=True)
        kpos = kb * tq + _iota((1, tq), 1)
        sel = ((blk > thr) | (eq & (before < need))) & (kpos <= qpos)
        sel_scr[kb] = jnp.where(sel, 0.0, NEG)

    outs = []
    for j in range(D_ATT // LANES):
        sl = slice(j * LANES, (j + 1) * LANES)

        def k_of(kb, sl=sl):
            return k_ref[0, pl.ds(pl.multiple_of(kb * tq, tq), tq), sl]

        def v_of(kb, sl=sl):
            return v_ref[0, pl.ds(pl.multiple_of(kb * tq, tq), tq), sl]

        def bias_of(kb, which):
            return sel_scr[kb]

        outs.append(_flash_pair(q_ref[0, :, sl], k_of, v_of, bias_of, nkb, tq))
    o_ref[0] = jnp.concatenate(outs, axis=1).astype(BF16)


def _dsa_prompt(u, kid, q, k, v, tq, topk):
    b, t, _ = q.shape
    nt = t // tq
    return pl.pallas_call(
        functools.partial(_dsa_prompt_kernel, tq=tq, topk=topk), grid=(b, nt),
        in_specs=[pl.BlockSpec((1, tq, BLK), lambda i, m: (i, m, BLK_IQ)),
                  pl.BlockSpec((1, tq, LANES), lambda i, m: (i, m, BLK_LAST * (BLK // LANES) + 1)),
                  pl.BlockSpec((1, t, LANES), lambda i, m: (i, 0, 0)),
                  pl.BlockSpec((1, tq, D_ATT), lambda i, m: (i, m, 0)),
                  pl.BlockSpec((1, t, D_ATT), lambda i, m: (i, 0, 0)),
                  pl.BlockSpec((1, t, D_ATT), lambda i, m: (i, 0, 0))],
        out_specs=pl.BlockSpec((1, tq, D_ATT), lambda i, m: (i, m, 0)),
        out_shape=jax.ShapeDtypeStruct((b, t, D_ATT), BF16),
        scratch_shapes=[pltpu.VMEM((nt, tq, tq), I32), pltpu.VMEM((nt, tq, tq), F32)],
        compiler_params=_cp("arbitrary", "arbitrary"), name="dsa_prompt",
    )(u, u, kid, q, k, v)


def _fox_bias_kernel(pt_ref, lfc_ref, nlf_ref, o_ref, rnew_scr, run_scr, *, t_new):
    s = pl.program_id(1)
    P = LANES
    strict = _tri(P, lambda r, c: r > c)

    @pl.when(s == 0)
    def _():
        nlf = nlf_ref[0]
        r = _dot3_r(nlf, strict)
        rnew_scr[...] = r
        run_scr[...] = jnp.broadcast_to(jnp.sum(nlf, axis=1, keepdims=True), run_scr.shape)
        lane = _iota((N_HEADS, P), 1)
        for t in range(t_new):
            blk = r - r[:, t:t + 1]
            o_ref[0, 0, t * N_HEADS:(t + 1) * N_HEADS, :] = jnp.where(lane <= t, blk, NEG)

    @pl.when(s > 0)
    def _():
        lf = lfc_ref[0, 0]
        suf = _dot3_r(lf, strict) + run_scr[...]
        r = rnew_scr[...]
        for t in range(t_new):
            o_ref[0, 0, t * N_HEADS:(t + 1) * N_HEADS, :] = suf - r[:, t:t + 1]
        run_scr[...] = run_scr[...] + jnp.sum(lf, axis=1, keepdims=True)


def _fox_bias(pt, lfc_t, layer, nlf, t_new):
    b, npg = pt.shape
    rows = t_new * N_HEADS

    def page(i, s, pt):
        return (layer, pt[i, jnp.minimum(npg - s, npg - 1)], 0, 0)

    return pl.pallas_call(
        functools.partial(_fox_bias_kernel, t_new=t_new),
        grid_spec=pltpu.PrefetchScalarGridSpec(
            num_scalar_prefetch=1, grid=(b, npg + 1),
            in_specs=[pl.BlockSpec((1, 1, N_HEADS, LANES), page),
                      pl.BlockSpec((1, N_HEADS, LANES), lambda i, s, pt: (i, 0, 0))],
            out_specs=pl.BlockSpec((1, 1, rows, LANES), lambda i, s, pt: (i, npg - s, 0, 0)),
            scratch_shapes=[pltpu.VMEM((N_HEADS, LANES), F32), pltpu.VMEM((N_HEADS, LANES), F32)]),
        out_shape=jax.ShapeDtypeStruct((b, npg + 1, rows, LANES), F32),
        compiler_params=_cp("arbitrary", "arbitrary"), name="fox_bias",
    )(pt, lfc_t, nlf)


def _idx_score_kernel(pt_ref, kic_ref, kin_ref, iq_ref, w_ref, o_ref, *, t_new):
    s = pl.program_id(1)
    iq = (iq_ref[0] * (HEAD_DIM ** -0.5)).astype(BF16)
    w = w_ref[0] * (N_HEADS ** -0.5)

    def scores(ki):
        d = jnp.maximum(_dot_nt(iq, ki.astype(BF16)), 0.0) * w
        return jnp.sum(d.reshape(t_new, N_HEADS, LANES), axis=1)

    pad = jnp.full((8 - t_new, LANES), -jnp.inf, F32)

    @pl.when(s == 0)
    def _():
        sc = scores(kin_ref[0])
        lane = _iota((t_new, LANES), 1)
        row = _iota((t_new, LANES), 0)
        o_ref[0, 0] = jnp.concatenate([jnp.where(lane <= row, sc, -jnp.inf), pad], axis=0)

    @pl.when(s > 0)
    def _():
        o_ref[0, 0] = jnp.concatenate([scores(kic_ref[0, 0]), pad], axis=0)


def _idx_scores(pt, kic, layer, ki_new, iq, w, t_new):
    b, npg = pt.shape
    rows = t_new * N_HEADS

    def page(i, s, pt):
        return (layer, pt[i, jnp.minimum(npg - s, npg - 1)], 0, 0)

    return pl.pallas_call(
        functools.partial(_idx_score_kernel, t_new=t_new),
        grid_spec=pltpu.PrefetchScalarGridSpec(
            num_scalar_prefetch=1, grid=(b, npg + 1),
            in_specs=[pl.BlockSpec((1, 1, LANES, HEAD_DIM), page),
                      pl.BlockSpec((1, LANES, HEAD_DIM), lambda i, s, pt: (i, 0, 0)),
                      pl.BlockSpec((1, rows, HEAD_DIM), lambda i, s, pt: (i, 0, 0)),
                      pl.BlockSpec((1, rows, 1), lambda i, s, pt: (i, 0, 0))],
            out_specs=pl.BlockSpec((1, 1, 8, LANES), lambda i, s, pt: (i, npg - s, 0, 0))),
        out_shape=jax.ShapeDtypeStruct((b, npg + 1, 8, LANES), F32),
        compiler_params=_cp("arbitrary", "arbitrary"), name="dsa_idx_scores",
    )(pt, kic, ki_new, iq, w)


def _select_kernel(sc_ref, o_ref, key_scr, *, t_new, topk):
    npg1 = sc_ref.shape[1]
    key_scr[...] = _sort_key(sc_ref[0])

    def count(fn):
        part = jnp.sum(jnp.where(fn(key_scr[...]), 1.0, 0.0), axis=0)
        return jnp.sum(part, axis=1, keepdims=True)

    kf = float(topk)
    thr = _kth_largest(lambda cand: count(lambda k: k >= cand), 8, kf)
    need = kf - count(lambda k: k > thr)
    strict = _tri(LANES, lambda r, c: r < c)

    def body(p, run):
        blk = key_scr[p]
        eq = blk == thr
        eqf = jnp.where(eq, 1.0, 0.0)
        before = _dot(eqf.astype(BF16), strict) + run
        sel = (blk > thr) | (eq & (before < need))
        bias = jnp.where(sel & (blk > _sort_key(jnp.full(blk.shape, -jnp.inf, F32))), 0.0, NEG)
        for t in range(t_new):
            o_ref[0, p, t * N_HEADS:(t + 1) * N_HEADS, :] = jnp.broadcast_to(bias[t:t + 1, :], (N_HEADS, LANES))
        return run + jnp.sum(eqf, axis=1, keepdims=True)

    lax.fori_loop(0, npg1, body, jnp.zeros((8, 1), F32))


def _dsa_select(scores, t_new, topk):
    b, npg1, _, _ = scores.shape
    rows = t_new * N_HEADS
    return pl.pallas_call(
        functools.partial(_select_kernel, t_new=t_new, topk=topk), grid=(b,),
        in_specs=[pl.BlockSpec((1, npg1, 8, LANES), lambda i: (i, 0, 0, 0))],
        out_specs=pl.BlockSpec((1, npg1, rows, LANES), lambda i: (i, 0, 0, 0)),
        out_shape=jax.ShapeDtypeStruct((b, npg1, rows, LANES), F32),
        scratch_shapes=[pltpu.VMEM((npg1, 8, LANES), I32)],
        compiler_params=_cp("arbitrary"), name="dsa_select",
    )(scores)


def _decode_kernel(pt_ref, q_ref, kc_ref, vc_ref, kn_ref, vn_ref, b_ref, o_ref,
                   qbd_scr, m_scr, l_scr, acc_scr, *, t_new, nsteps):
    s = pl.program_id(1)
    rows = t_new * N_HEADS
    lane = _iota((N_HEADS, D_ATT), 1)
    hrow = _iota((N_HEADS, D_ATT), 0)
    hmask = (lane >= hrow * HEAD_DIM) & (lane < (hrow + 1) * HEAD_DIM)

    @pl.when(s == 0)
    def _():
        q = q_ref[0].astype(F32) * (HEAD_DIM ** -0.5)
        for t in range(t_new):
            qt = jnp.broadcast_to(q[t:t + 1, :], (N_HEADS, D_ATT))
            qbd_scr[t * N_HEADS:(t + 1) * N_HEADS, :] = jnp.where(hmask, qt, 0.0).astype(BF16)
        m_scr[...] = jnp.full(m_scr.shape, NEG, F32)
        l_scr[...] = jnp.zeros_like(l_scr)
        acc_scr[...] = jnp.zeros_like(acc_scr)

    def step(k, v):
        sc = _dot_nt(qbd_scr[...], k.astype(BF16)) + b_ref[0, 0]
        m_old = m_scr[...]
        m_new = jnp.maximum(m_old, jnp.max(sc, axis=1, keepdims=True))
        p = jnp.exp(sc - m_new)
        alpha = jnp.exp(m_old - m_new)
        l_scr[...] = alpha * l_scr[...] + jnp.sum(p, axis=1, keepdims=True)
        acc_scr[...] = alpha * acc_scr[...] + _dot(p.astype(BF16), v.astype(BF16))
        m_scr[...] = m_new

    @pl.when(s == 0)
    def _():
        step(kn_ref[0], vn_ref[0])

    @pl.when(s > 0)
    def _():
        step(kc_ref[0, 0], vc_ref[0, 0])

    @pl.when(s == nsteps - 1)
    def _():
        out = acc_scr[...] / l_scr[...]
        ys = []
        for t in range(t_new):
            blk = jnp.where(hmask, out[t * N_HEADS:(t + 1) * N_HEADS, :], 0.0)
            ys.append(jnp.sum(blk, axis=0, keepdims=True))
        o_ref[0] = jnp.concatenate(ys, axis=0).astype(BF16)


def _decode_attn(pt, q, kc, vc, layer, k_new, v_new, bias, t_new):
    b, npg = pt.shape
    rows = t_new * N_HEADS

    def page(i, s, pt):
        return (layer, pt[i, jnp.minimum(npg - s, npg - 1)], 0, 0)

    return pl.pallas_call(
        functools.partial(_decode_kernel, t_new=t_new, nsteps=npg + 1),
        grid_spec=pltpu.PrefetchScalarGridSpec(
            num_scalar_prefetch=1, grid=(b, npg + 1),
            in_specs=[pl.BlockSpec((1, t_new, D_ATT), lambda i, s, pt: (i, 0, 0)),
                      pl.BlockSpec((1, 1, LANES, D_ATT), page),
                      pl.BlockSpec((1, 1, LANES, D_ATT), page),
                      pl.BlockSpec((1, LANES, D_ATT), lambda i, s, pt: (i, 0, 0)),
                      pl.BlockSpec((1, LANES, D_ATT), lambda i, s, pt: (i, 0, 0)),
                      pl.BlockSpec((1, 1, rows, LANES), lambda i, s, pt: (i, npg - s, 0, 0))],
            out_specs=pl.BlockSpec((1, t_new, D_ATT), lambda i, s, pt: (i, 0, 0)),
            scratch_shapes=[pltpu.VMEM((rows, D_ATT), BF16), pltpu.VMEM((rows, 1), F32),
                            pltpu.VMEM((rows, 1), F32), pltpu.VMEM((rows, D_ATT), F32)]),
        out_shape=jax.ShapeDtypeStruct((b, t_new, D_ATT), BF16),
        compiler_params=_cp("arbitrary", "arbitrary"), name="decode_attn",
    )(pt, q, kc, vc, k_new, v_new, bias)


def _pack_w_in(w_in):
    o = 0
    seg = {}
    for name, n in (('z', 1024), ('xbc', 1536), ('dt', 16), ('fq', 512), ('fk', 512), ('fv', 512), ('ff', 8),
                    ('dq', 512), ('dk', 512), ('dv', 512), ('iq', 512), ('ik', 64), ('iw', 8)):
        seg[name] = w_in[:, :, o:o + n]
        o += n
    used = 1024 + 1536 + 7 * 512 + 64 + 64 + 16 + 8 + 8
    pad = jnp.zeros(w_in.shape[:2] + (N_PACK - used,), w_in.dtype)
    parts = [seg[k] for k in ('z', 'xbc', 'fq', 'fk', 'fv', 'dq', 'dk', 'dv', 'iq', 'ik', 'ik', 'dt', 'ff', 'iw')]
    return jnp.concatenate(parts + [pad], axis=-1).astype(BF16)


def _lane_row(vals, offset):
    return jnp.zeros((1, LANES), F32).at[0, offset:offset + vals.shape[0]].set(vals.astype(F32))


def _layer_consts(l, P):
    a = -jnp.exp(P['ssd_a_log'][l].astype(F32))
    dtb_row = _lane_row(P['ssd_dt_bias'][l], SM_DT)
    a_row = _lane_row(a, SM_DT)
    gmat = jnp.kron(jnp.eye(N_HEADS, dtype=F32), jnp.full((HEAD_DIM, HEAD_DIM), 1.0 / HEAD_DIM, F32)).astype(BF16)

    def tile_heads(g):
        return jnp.tile(g.astype(F32), N_HEADS).reshape(1, D_ATT)

    return dict(
        ssd_conv_w=P['ssd_conv_w'][l], ssd_conv_b=P['ssd_conv_b'][l].reshape(1, -1),
        dtb_row=dtb_row, dtb_col=dtb_row.reshape(LANES, 1), a_row=a_row, a_col=a_row.reshape(LANES, 1),
        d_exp=jnp.repeat(P['ssd_d'][l].astype(F32), HEAD_DIM).reshape(1, D_SSD),
        ssd_norm=P['ssd_norm'][l].reshape(1, D_SSD), gmat=gmat,
        fox_q_norm=tile_heads(P['fox_q_norm'][l]), fox_k_norm=tile_heads(P['fox_k_norm'][l]),
        dsa_q_norm=tile_heads(P['dsa_q_norm'][l]), dsa_k_norm=tile_heads(P['dsa_k_norm'][l]),
        fb_row=_lane_row(P['fox_f_bias'][l], SM_FF),
    )


def _pad_rows(x, rows):
    return jnp.pad(x, ((0, 0), (0, rows - x.shape[1]), (0, 0)))


def _layer(l, x, mod, P, wl, past):
    b, t, d = x.shape
    sh1, sc1, g1, sh2, sc2, g2 = [mod[:, None, i * d:(i + 1) * d] for i in range(6)]
    prompt = past is None
    m = b * t
    f2 = P['w_up'].shape[-1]

    h = _norm_mod(x, P['attn_norm'][l], sc1, sh1)
    if prompt:
        u = _matmul(h, P['w_in_p'][l], 512, BLK)
        uf = u
    else:
        uf = _matmul(h.reshape(1, m, d), P['w_in_p'][l], m, BLK)
        u = uf.reshape(b, t, N_PACK)

    if prompt:
        y_ssd, st = _ssd(u, wl, SSD_CHUNK)
    else:
        prev = jnp.pad(past['state_ssd_conv'][l], ((0, 0), (8 - (SSD_CONV - 1), 0), (0, 0)))
        h0 = past['state_ssd'][l].reshape(b, SSD_NHEADS * HEAD_DIM, SSD_STATE)
        y_ssd, st = _ssd(_pad_rows(u, SSD_CHUNK), wl, t, prev, h0)
        y_ssd = y_ssd[:, :t]
    ssd_new = st.reshape(b, SSD_NHEADS, HEAD_DIM, SSD_STATE)
    xbc_raw = u[:, :, BLK_Z * BLK + D_SSD:BLK_Z * BLK + D_SSD + 3 * BLK]
    if prompt:
        ssd_conv_new = xbc_raw[:, t - (SSD_CONV - 1):]
    else:
        ssd_conv_new = jnp.concatenate([past['state_ssd_conv'][l], xbc_raw], axis=1)[:, t:]

    tq = 256
    (fqn, fkn, fkb, fvb, dqn, dkn, dkb, dvb, kid, logf, *cum) = _prep(uf, wl, tq, prompt)
    fk_out = fkn.reshape(b, t, N_HEADS, HEAD_DIM)
    dk_out = dkn.reshape(b, t, N_HEADS, HEAD_DIM)
    fv_out = u[:, :, BLK_FV * BLK:(BLK_FV + 1) * BLK].reshape(b, t, N_HEADS, HEAD_DIM)
    dv_out = u[:, :, BLK_DV * BLK:(BLK_DV + 1) * BLK].reshape(b, t, N_HEADS, HEAD_DIM)
    ik_out = u[:, :, BLK_LAST * BLK:BLK_LAST * BLK + HEAD_DIM]
    logf_out = logf.reshape(b, t, LANES)[:, :, SM_FF:SM_FF + N_HEADS]

    if prompt:
        cumc, cumr = cum
        y_fox = _fox_prompt(fqn, fkb, fvb, cumc, cumr, tq)
        y_dsa = _dsa_prompt(u, kid, dqn, dkb, dvb, tq, min(TOPK_MAX, t // 4))
    else:
        pt = past['page_table']
        npg = pt.shape[1]
        page = past['cache_fox_k'].shape[2]
        L_all = npg * page + t

        def r3(a):
            return a.reshape(b, t, -1)

        def newpage(a):
            return _pad_rows(r3(a), page)

        nlf = jnp.pad(jnp.swapaxes(logf_out, 1, 2), ((0, 0), (0, 0), (0, LANES - t)))
        bias_f = _fox_bias(pt, past['fox_logf_t'], l, nlf, t)
        y_fox = _decode_attn(pt, r3(fqn), past['fox_k4'], past['fox_v4'], l, newpage(fkb), newpage(fvb), bias_f, t)

        iq = u[:, :, BLK_IQ * BLK:(BLK_IQ + 1) * BLK].reshape(b, t * N_HEADS, HEAD_DIM)
        iw = u[:, :, BLK_LAST * BLK + LANES + SM_IW:BLK_LAST * BLK + LANES + SM_IW + N_HEADS]
        sc = _idx_scores(pt, past['cache_dsa_kidx'], l, _pad_rows(ik_out, page), iq,
                         iw.reshape(b, t * N_HEADS, 1), t)
        bias_d = _dsa_select(sc, t, min(TOPK_MAX, L_all // 4))
        y_dsa = _decode_attn(pt, r3(dqn), past['dsa_k4'], past['dsa_v4'], l, newpage(dkb), newpage(dvb), bias_d, t)

    mixed = jnp.concatenate([y_ssd, y_fox.reshape(b, t, D_ATT), y_dsa.reshape(b, t, D_ATT)], axis=-1)
    if prompt:
        x = _matmul(mixed, P['w_out_b'][l], 512, 512, res=x, gate=g1)
    else:
        x = _matmul(mixed.reshape(1, m, d), P['w_out_b'][l], m, 512, res=x.reshape(1, m, d),
                    gate=jnp.broadcast_to(g1, (b, t, d)).reshape(1, m, d)).reshape(b, t, d)

    h2 = _norm_mod(x, P['ffn_norm'][l], sc2, sh2)
    cw = P['ffn_conv_w'][l]
    cb = P['ffn_conv_b'][l].reshape(1, f2)
    if prompt:
        act, ta, tg = _ffn_up_prompt(h2.reshape(m, d), P['w_up_b'][l], cw, cb, t)
        ffn_conv_new = jnp.concatenate([ta, tg], axis=-1)[:, 8 - (FFN_CONV - 1):]
        x = _matmul(act.reshape(b, t, -1), P['w_down_b'][l], 512, 512, res=x, gate=g2)
    else:
        buf = past['state_ffn_conv'][l]
        zero = jnp.zeros((b, t - 1, f2), F32)
        p1 = jnp.concatenate([buf[:, 1:2], zero], axis=1).reshape(m, f2)
        p2 = jnp.concatenate([buf[:, 0:2], zero[:, 1:]], axis=1).reshape(m, f2)
        act, ra, rg = _ffn_up_sample(h2.reshape(m, d), P['w_up_b'][l], cw, cb, p1, p2, t)
        raw = jnp.concatenate([ra, rg], axis=-1).reshape(b, t, f2)
        ffn_conv_new = jnp.concatenate([buf, raw], axis=1)[:, t:]
        x = _matmul(act.reshape(1, m, -1), P['w_down_b'][l], m, 512, res=x.reshape(1, m, d),
                    gate=jnp.broadcast_to(g2, (b, t, d)).reshape(1, m, d)).reshape(b, t, d)

    new = (fk_out, fv_out, logf_out, dk_out, dv_out, ik_out, ssd_new, ssd_conv_new, ffn_conv_new)
    return x, new


def kernel(x_prompt, x_sample, cache_fox_k, cache_fox_v, cache_fox_logf, cache_dsa_k, cache_dsa_v, cache_dsa_kidx, state_ssd, state_ssd_conv, state_ffn_conv, page_table, c_prompt, c_sample, w_ada, b_ada, attn_norm, w_in, ssd_conv_w, ssd_conv_b, ssd_dt_bias, ssd_a_log, ssd_d, ssd_norm, fox_q_norm, fox_k_norm, fox_f_bias, dsa_q_norm, dsa_k_norm, w_out, ffn_norm, w_up, ffn_conv_w, ffn_conv_b, w_down):
    depth = w_in.shape[0]
    bp = x_prompt.shape[0]
    bs = x_sample.shape[0]
    P = dict(attn_norm=attn_norm, ssd_conv_w=ssd_conv_w, ssd_conv_b=ssd_conv_b, ssd_dt_bias=ssd_dt_bias,
             ssd_a_log=ssd_a_log, ssd_d=ssd_d, ssd_norm=ssd_norm, fox_q_norm=fox_q_norm, fox_k_norm=fox_k_norm,
             fox_f_bias=fox_f_bias, dsa_q_norm=dsa_q_norm, dsa_k_norm=dsa_k_norm, ffn_norm=ffn_norm,
             ffn_conv_w=ffn_conv_w, ffn_conv_b=ffn_conv_b, w_up=w_up,
             w_in_p=_pack_w_in(w_in), w_out_b=w_out.astype(BF16), w_up_b=w_up.astype(BF16),
             w_down_b=w_down.astype(BF16))
    npool, page = cache_fox_k.shape[1], cache_fox_k.shape[2]
    past = dict(state_ssd=state_ssd, state_ssd_conv=state_ssd_conv, state_ffn_conv=state_ffn_conv,
                page_table=page_table, cache_fox_k=cache_fox_k, cache_dsa_kidx=cache_dsa_kidx,
                fox_k4=cache_fox_k.reshape(depth, npool, page, D_ATT),
                fox_v4=cache_fox_v.reshape(depth, npool, page, D_ATT),
                dsa_k4=cache_dsa_k.reshape(depth, npool, page, D_ATT),
                dsa_v4=cache_dsa_v.reshape(depth, npool, page, D_ATT),
                fox_logf_t=jnp.swapaxes(cache_fox_logf, 2, 3))

    rows = bp + bs
    rows_pad = -(-rows // 8) * 8
    c_all = jnp.pad(jnp.concatenate([c_prompt, c_sample], axis=0), ((0, rows_pad - rows), (0, 0)))
    mods = _ada_mod(c_all, w_ada, b_ada)

    xp, xs = x_prompt, x_sample
    new_p, new_s = [], []
    for l in range(depth):
        wl = _layer_consts(l, P)
        xp, sp = _layer(l, xp, mods[l, :bp], P, wl, None)
        xs, ss = _layer(l, xs, mods[l, bp:rows], P, wl, past)
        new_p.append(sp)
        new_s.append(ss)
    outs_p = [jnp.stack(a, 0) for a in zip(*new_p)]
    outs_s = [jnp.stack(a, 0) for a in zip(*new_s)]
    return tuple([xp, xs] + outs_p + outs_s)
```

```python
import functools

import jax
import jax.numpy as jnp
from jax import lax
from jax.experimental import pallas as pl
from jax.experimental.pallas import tpu as pltpu

F32 = jnp.float32
BF16 = jnp.bfloat16
I32 = jnp.int32

RMS_EPS = 1e-6
NEG = -1e30
HEAD_DIM = 64
LANES = 128
D_SSD = 1024
SSD_STATE = 128
SSD_NHEADS = 16
SSD_CONV = 4
SSD_CHUNK = 128
D_ATT = 512
N_HEADS = 8
FFN_CONV = 3
TOPK_MAX = 256
VMEM_LIMIT = 56 * 1024 * 1024
PAGES_PER_STEP_KV = 8
PAGES_PER_STEP_AUX = 16

BLK = 512
BLK_Z, BLK_X, BLK_FQ, BLK_FK, BLK_FV, BLK_DQ, BLK_DK, BLK_DV, BLK_IQ, BLK_LAST = 0, 2, 5, 6, 7, 8, 9, 10, 11, 12
N_PACK = 13 * BLK
SM_DT, SM_FF, SM_IW = 0, 16, 24


def _cp(*sem):
    return pltpu.CompilerParams(dimension_semantics=sem, vmem_limit_bytes=VMEM_LIMIT)


def _dot(a, b):
    return jnp.dot(a, b, preferred_element_type=F32)


def _dot_nt(a, b):
    return lax.dot_general(a, b, (((1,), (1,)), ((), ())), preferred_element_type=F32)


def _split3(x):
    hi = x.astype(BF16)
    r = x - hi.astype(F32)
    mid = r.astype(BF16)
    lo = (r - mid.astype(F32)).astype(BF16)
    return hi, mid, lo


def _dot3_l(m01, x):
    hi, mid, lo = _split3(x)
    return _dot(m01, hi) + _dot(m01, mid) + _dot(m01, lo)


def _dot3_r(x, m01):
    hi, mid, lo = _split3(x)
    return _dot(hi, m01) + _dot(mid, m01) + _dot(lo, m01)


def _softplus(x):
    return jnp.maximum(x, 0.0) + jnp.log1p(jnp.exp(-jnp.abs(x)))


def _silu(x):
    return x * jax.nn.sigmoid(x)


def _iota(shape, dim):
    return lax.broadcasted_iota(I32, shape, dim)


def _tri(n, fn):
    r = _iota((n, n), 0)
    c = _iota((n, n), 1)
    return jnp.where(fn(r, c), 1.0, 0.0).astype(BF16)


def _ada_kernel(c_ref, w_ref, b_ref, o_ref):
    s = _silu(c_ref[...]).astype(BF16)
    o_ref[0] = _dot(s, w_ref[0].astype(BF16)) + b_ref[0]


def _ada_mod(c_all, w_ada, b_ada):
    nl, d, n = w_ada.shape
    rows = c_all.shape[0]
    tn = 1024
    return pl.pallas_call(
        _ada_kernel, grid=(nl, n // tn),
        in_specs=[pl.BlockSpec((rows, d), lambda l, j: (0, 0)),
                  pl.BlockSpec((1, d, tn), lambda l, j: (l, 0, j)),
                  pl.BlockSpec((1, 1, tn), lambda l, j: (l, 0, j))],
        out_specs=pl.BlockSpec((1, rows, tn), lambda l, j: (l, 0, j)),
        out_shape=jax.ShapeDtypeStruct((nl, rows, n), F32),
        compiler_params=_cp("arbitrary", "arbitrary"), name="ada_mod",
    )(c_all, w_ada, b_ada.reshape(nl, 1, n))


def _normmod_kernel(x_ref, g_ref, sc_ref, sh_ref, o_ref):
    x = x_ref[0]
    ms = jnp.mean(x * x, axis=-1, keepdims=True)
    y = x * lax.rsqrt(ms + RMS_EPS) * g_ref[...]
    o_ref[0] = (y * (1.0 + sc_ref[0]) + sh_ref[0]).astype(BF16)


def _norm_mod(x, g, sc, sh):
    b, t, d = x.shape
    tm = min(t, 512)
    return pl.pallas_call(
        _normmod_kernel, grid=(b, t // tm),
        in_specs=[pl.BlockSpec((1, tm, d), lambda i, m: (i, m, 0)),
                  pl.BlockSpec((1, d), lambda i, m: (0, 0)),
                  pl.BlockSpec((1, 1, d), lambda i, m: (i, 0, 0)),
                  pl.BlockSpec((1, 1, d), lambda i, m: (i, 0, 0))],
        out_specs=pl.BlockSpec((1, tm, d), lambda i, m: (i, m, 0)),
        out_shape=jax.ShapeDtypeStruct((b, t, d), BF16),
        compiler_params=_cp("arbitrary", "arbitrary"), name="norm_mod",
    )(x, g.reshape(1, d), sc, sh)


def _mm_kernel(x_ref, w_ref, o_ref):
    o_ref[0] = _dot(x_ref[0], w_ref[...])


def _mm_res_kernel(x_ref, w_ref, r_ref, g_ref, o_ref):
    o_ref[0] = r_ref[0] + g_ref[0] * _dot(x_ref[0], w_ref[...])


def _matmul(x, w, tm, tn, res=None, gate=None):
    g, t, k = x.shape
    n = w.shape[1]
    tm = min(tm, t)
    grid = (n // tn, g, t // tm)
    x_spec = pl.BlockSpec((1, tm, k), lambda j, i, m: (i, m, 0))
    w_spec = pl.BlockSpec((k, tn), lambda j, i, m: (0, j))
    o_spec = pl.BlockSpec((1, tm, tn), lambda j, i, m: (i, m, j))
    out_shape = jax.ShapeDtypeStruct((g, t, n), F32)
    cp = _cp("arbitrary", "arbitrary", "arbitrary")
    if res is None:
        return pl.pallas_call(_mm_kernel, grid=grid, in_specs=[x_spec, w_spec], out_specs=o_spec,
                              out_shape=out_shape, compiler_params=cp, name="proj")(x, w)
    if gate.shape[1] == 1:
        g_spec = pl.BlockSpec((1, 1, tn), lambda j, i, m: (i, 0, j))
    else:
        g_spec = pl.BlockSpec((1, tm, tn), lambda j, i, m: (i, m, j))
    return pl.pallas_call(_mm_res_kernel, grid=grid, in_specs=[x_spec, w_spec, o_spec, g_spec],
                          out_specs=o_spec, out_shape=out_shape, compiler_params=cp,
                          name="proj_res")(x, w, res, gate)


def _conv3(u, prev1, prev2, cw_ref, cb_ref):
    return cb_ref[...] + prev2 * cw_ref[0:1, :] + prev1 * cw_ref[1:2, :] + u * cw_ref[2:3, :]


def _shift_with_carry(u, carry):
    r1 = pltpu.roll(u, 1, 0)
    r2 = pltpu.roll(u, 2, 0)
    c1 = pltpu.roll(carry, 1, 0)
    c2 = pltpu.roll(carry, 2, 0)
    row = _iota((8, u.shape[1]), 0)
    p1 = jnp.concatenate([jnp.where(row < 1, c1, r1[0:8]), r1[8:]], axis=0)
    p2 = jnp.concatenate([jnp.where(row < 2, c2, r2[0:8]), r2[8:]], axis=0)
    return p1, p2


def _ffn_up_prompt_kernel(x_ref, wa_ref, wg_ref, cwa_ref, cwg_ref, cba_ref, cbg_ref,
                          act_ref, ta_ref, tg_ref, ca_scr, cg_scr, *, tiles_per_seq):
    m = pl.program_id(1)
    tm = x_ref.shape[0]

    @pl.when(m % tiles_per_seq == 0)
    def _():
        ca_scr[...] = jnp.zeros_like(ca_scr)
        cg_scr[...] = jnp.zeros_like(cg_scr)

    x = x_ref[...]
    ua = _dot(x, wa_ref[...])
    ug = _dot(x, wg_ref[...])
    pa1, pa2 = _shift_with_carry(ua, ca_scr[...])
    pg1, pg2 = _shift_with_carry(ug, cg_scr[...])
    a = _conv3(ua, pa1, pa2, cwa_ref, cba_ref)
    gt = _conv3(ug, pg1, pg2, cwg_ref, cbg_ref)
    act_ref[...] = (_silu(gt) * a).astype(BF16)
    ca_scr[...] = ua[tm - 8:tm]
    cg_scr[...] = ug[tm - 8:tm]
    ta_ref[0] = ua[tm - 8:tm]
    tg_ref[0] = ug[tm - 8:tm]


def _ffn_up_prompt(h, w_up, cw, cb, t_seq):
    m, k = h.shape
    f = w_up.shape[1] // 2
    tm, tn = 512, 512
    nj = f // tn
    nb = m // t_seq
    tps = t_seq // tm
    cw_spec_a = pl.BlockSpec((FFN_CONV, tn), lambda j, i: (0, j))
    cw_spec_g = pl.BlockSpec((FFN_CONV, tn), lambda j, i: (0, j + nj))
    cb_spec_a = pl.BlockSpec((1, tn), lambda j, i: (0, j))
    cb_spec_g = pl.BlockSpec((1, tn), lambda j, i: (0, j + nj))
    tail_spec = pl.BlockSpec((1, 8, tn), lambda j, i: (i // tps, 0, j))
    return pl.pallas_call(
        functools.partial(_ffn_up_prompt_kernel, tiles_per_seq=tps),
        grid=(nj, m // tm),
        in_specs=[pl.BlockSpec((tm, k), lambda j, i: (i, 0)),
                  pl.BlockSpec((k, tn), lambda j, i: (0, j)),
                  pl.BlockSpec((k, tn), lambda j, i: (0, j + nj)),
                  cw_spec_a, cw_spec_g, cb_spec_a, cb_spec_g],
        out_specs=[pl.BlockSpec((tm, tn), lambda j, i: (i, j)), tail_spec, tail_spec],
        out_shape=[jax.ShapeDtypeStruct((m, f), BF16),
                   jax.ShapeDtypeStruct((nb, 8, f), F32),
                   jax.ShapeDtypeStruct((nb, 8, f), F32)],
        scratch_shapes=[pltpu.VMEM((8, tn), F32), pltpu.VMEM((8, tn), F32)],
        compiler_params=_cp("arbitrary", "arbitrary"), name="ffn_up_prompt",
    )(h, w_up, w_up, cw, cw, cb, cb)


def _ffn_up_sample_kernel(x_ref, wa_ref, wg_ref, cwa_ref, cwg_ref, cba_ref, cbg_ref,
                          p1a_ref, p2a_ref, p1g_ref, p2g_ref, act_ref, ra_ref, rg_ref, *, t_seq):
    x = x_ref[...]
    ua = _dot(x, wa_ref[...])
    ug = _dot(x, wg_ref[...])
    t = _iota(ua.shape, 0) % t_seq

    def prevs(u, p1_ref, p2_ref):
        p1 = jnp.where(t >= 1, pltpu.roll(u, 1, 0), 0.0) + p1_ref[...]
        p2 = jnp.where(t >= 2, pltpu.roll(u, 2, 0), 0.0) + p2_ref[...]
        return p1, p2

    pa1, pa2 = prevs(ua, p1a_ref, p2a_ref)
    pg1, pg2 = prevs(ug, p1g_ref, p2g_ref)
    a = _conv3(ua, pa1, pa2, cwa_ref, cba_ref)
    gt = _conv3(ug, pg1, pg2, cwg_ref, cbg_ref)
    act_ref[...] = (_silu(gt) * a).astype(BF16)
    ra_ref[...] = ua
    rg_ref[...] = ug


def _ffn_up_sample(h, w_up, cw, cb, p1, p2, t_seq):
    m, k = h.shape
    f = w_up.shape[1] // 2
    tn = 512
    nj = f // tn
    a_spec = pl.BlockSpec((m, tn), lambda j: (0, j))
    g_spec = pl.BlockSpec((m, tn), lambda j: (0, j + nj))
    return pl.pallas_call(
        functools.partial(_ffn_up_sample_kernel, t_seq=t_seq),
        grid=(nj,),
        in_specs=[pl.BlockSpec((m, k), lambda j: (0, 0)),
                  pl.BlockSpec((k, tn), lambda j: (0, j)),
                  pl.BlockSpec((k, tn), lambda j: (0, j + nj)),
                  pl.BlockSpec((FFN_CONV, tn), lambda j: (0, j)),
                  pl.BlockSpec((FFN_CONV, tn), lambda j: (0, j + nj)),
                  pl.BlockSpec((1, tn), lambda j: (0, j)),
                  pl.BlockSpec((1, tn), lambda j: (0, j + nj)),
                  a_spec, a_spec, g_spec, g_spec],
        out_specs=[a_spec, a_spec, a_spec],
        out_shape=[jax.ShapeDtypeStruct((m, f), BF16),
                   jax.ShapeDtypeStruct((m, f), F32),
                   jax.ShapeDtypeStruct((m, f), F32)],
        compiler_params=_cp("arbitrary"), name="ffn_up_sample",
    )(h, w_up, w_up, cw, cw, cb, cb, p1, p2, p1, p2)


def _ssd_kernel(*refs, valid, has_init, nchunks):
    if has_init:
        (z0, z1, x0, x1, x2, sm_ref, cw_ref, cb_ref, dtb_row, dtb_col, a_row, a_col, d_ref, ng_ref,
         prev_ref, h0_ref, y_ref, st_ref, xx_scr, st_scr) = refs
    else:
        (z0, z1, x0, x1, x2, sm_ref, cw_ref, cb_ref, dtb_row, dtb_col, a_row, a_col, d_ref, ng_ref,
         y_ref, st_ref, xx_scr, st_scr) = refs
    c = pl.program_id(1)
    L = SSD_CHUNK
    npair = SSD_NHEADS // 2

    @pl.when(c == 0)
    def _():
        if has_init:
            xx_scr[0:8, :] = prev_ref[0]
            for j in range(npair):
                st_scr[j] = h0_ref[0, j * L:(j + 1) * L, :].T
        else:
            xx_scr[0:8, :] = jnp.zeros((8, xx_scr.shape[1]), F32)
            st_scr[...] = jnp.zeros_like(st_scr)

    xx_scr[8:8 + L, 0:BLK] = x0[0]
    xx_scr[8:8 + L, BLK:2 * BLK] = x1[0]
    xx_scr[8:8 + L, 2 * BLK:3 * BLK] = x2[0]
    acc = cb_ref[...] + xx_scr[5:5 + L, :] * cw_ref[0:1, :]
    for i in range(1, SSD_CONV):
        acc = acc + xx_scr[5 + i:5 + i + L, :] * cw_ref[i:i + 1, :]
    xx_scr[0:8, :] = xx_scr[L:L + 8, :]
    xbc = _silu(acc)
    xs = xbc[:, 0:D_SSD]
    bm = xbc[:, D_SSD:D_SSD + 2 * SSD_STATE]
    cm = xbc[:, D_SSD + 2 * SSD_STATE:D_SSD + 4 * SSD_STATE]

    small = sm_ref[0]
    dt_c = _softplus(small + dtb_row[...])
    small_t = small.T
    dt_r = _softplus(small_t + dtb_col[...])
    if valid < L:
        dt_c = jnp.where(_iota((L, LANES), 0) < valid, dt_c, 0.0)
        dt_r = jnp.where(_iota((LANES, L), 1) < valid, dt_r, 0.0)
    a_c = dt_c * a_row[...]
    a_r = dt_r * a_col[...]
    tril = _tri(L, lambda r, cc: r >= cc)
    triu = _tri(L, lambda r, cc: r <= cc)
    ones = jnp.ones((L, L), BF16)
    cs_c = _dot3_l(tril, a_c)
    cs_r = _dot3_r(a_r, triu)
    tot_c = _dot3_l(ones, a_c)
    e_cs = jnp.exp(cs_c)
    e_dec = jnp.exp(tot_c - cs_c)
    e_tot = jnp.exp(tot_c)

    lane = _iota((L, LANES), 1)
    first = lane < HEAD_DIM
    causal = _iota((L, L), 0) >= _iota((L, L), 1)

    def bcast(mat, h):
        return jnp.broadcast_to(mat[:, h:h + 1], (L, LANES))

    def pair(mat, h):
        return jnp.where(first, bcast(mat, h), bcast(mat, h + 1))

    ys = []
    for g in range(2):
        b_g = bm[:, g * SSD_STATE:(g + 1) * SSD_STATE]
        c_g = cm[:, g * SSD_STATE:(g + 1) * SSD_STATE].astype(BF16)
        cb_g = _dot_nt(c_g, b_g.astype(BF16))
        bt_g = b_g.T.astype(BF16)
        for jj in range(npair // 2):
            j = g * (npair // 2) + jj
            ha = 2 * j
            xs_p = xs[:, j * LANES:(j + 1) * LANES]
            x_p = xs_p * pair(dt_c, ha)
            x_pb = x_p.astype(BF16)
            yd = []
            for h in (ha, ha + 1):
                diff = jnp.minimum(bcast(cs_c, h) - cs_r[h:h + 1, :], 0.0)
                lm = jnp.where(causal, jnp.exp(diff), 0.0)
                yd.append(_dot((cb_g * lm).astype(BF16), x_pb))
            y_diag = jnp.where(first, yd[0], yd[1])
            st_old = st_scr[j]
            y_off = _dot(c_g, st_old.astype(BF16)) * pair(e_cs, ha)
            ys.append(y_diag + y_off + d_ref[:, j * LANES:(j + 1) * LANES] * xs_p)
            xd = (x_p * pair(e_dec, ha)).astype(BF16)
            st_scr[j] = pair(e_tot, ha) * st_old + _dot(bt_g, xd)
    y = jnp.concatenate(ys, axis=1)

    zg = jnp.concatenate([z0[0], z1[0]], axis=1)
    y = y * _silu(zg)
    half = D_SSD // 2
    outs = []
    for g in range(2):
        yg = y[:, g * half:(g + 1) * half]
        ms = jnp.mean(yg * yg, axis=-1, keepdims=True)
        outs.append(yg * lax.rsqrt(ms + RMS_EPS) * ng_ref[:, g * half:(g + 1) * half])
    y_ref[0] = jnp.concatenate(outs, axis=1).astype(BF16)

    @pl.when(c == nchunks - 1)
    def _():
        for j in range(npair):
            st_ref[0, j * L:(j + 1) * L, :] = st_scr[j].T


def _ssd(u, wl, valid, prev=None, h0=None):
    b, t, _ = u.shape
    L = SSD_CHUNK
    nch = t // L
    has_init = prev is not None

    def ublk(k):
        return pl.BlockSpec((1, L, BLK), lambda i, c: (i, c, k))

    def full(a):
        return pl.BlockSpec(a.shape, lambda i, c: (0,) * a.ndim)

    consts = [wl['ssd_conv_w'], wl['ssd_conv_b'], wl['dtb_row'], wl['dtb_col'], wl['a_row'], wl['a_col'],
              wl['d_exp'], wl['ssd_norm']]
    in_specs = [ublk(BLK_Z), ublk(BLK_Z + 1), ublk(BLK_X), ublk(BLK_X + 1), ublk(BLK_X + 2),
                pl.BlockSpec((1, L, LANES), lambda i, c: (i, c, BLK_LAST * (BLK // LANES) + 1))]
    in_specs += [full(a) for a in consts]
    args = [u, u, u, u, u, u] + consts
    if has_init:
        in_specs += [pl.BlockSpec((1, 8, 3 * BLK), lambda i, c: (i, 0, 0)),
                     pl.BlockSpec((1, SSD_NHEADS * HEAD_DIM, SSD_STATE), lambda i, c: (i, 0, 0))]
        args += [prev, h0]
    return pl.pallas_call(
        functools.partial(_ssd_kernel, valid=valid, has_init=has_init, nchunks=nch),
        grid=(b, nch), in_specs=in_specs,
        out_specs=[pl.BlockSpec((1, L, D_SSD), lambda i, c: (i, c, 0)),
                   pl.BlockSpec((1, SSD_NHEADS * HEAD_DIM, SSD_STATE), lambda i, c: (i, 0, 0))],
        out_shape=[jax.ShapeDtypeStruct((b, t, D_SSD), BF16),
                   jax.ShapeDtypeStruct((b, SSD_NHEADS * HEAD_DIM, SSD_STATE), F32)],
        scratch_shapes=[pltpu.VMEM((L + 8, 3 * BLK), F32),
                        pltpu.VMEM((SSD_NHEADS // 2, SSD_STATE, LANES), F32)],
        compiler_params=_cp("arbitrary", "arbitrary"), name="ssd",
    )(*args)


def _prep_kernel(*refs, do_cum):
    (fq, fk, fv, dq, dk, dv, last, gm, gfq, gfk, gdq, gdk, fb) = refs[:13]
    if do_cum:
        (fqn, fkn, fkb, fvb, dqn, dkn, dkb, dvb, kid, logf_o, cumc, cumr, carry) = refs[13:]
    else:
        (fqn, fkn, fkb, fvb, dqn, dkn, dkb, dvb, kid, logf_o) = refs[13:]
    m = pl.program_id(1)
    gmat = gm[...]

    def headnorm(x, g_ref):
        sq = x * x
        hi = sq.astype(BF16)
        lo = (sq - hi.astype(F32)).astype(BF16)
        ms = _dot(hi, gmat) + _dot(lo, gmat)
        return x * lax.rsqrt(ms + RMS_EPS) * g_ref[...]

    fqn[0] = headnorm(fq[0], gfq).astype(BF16)
    k1 = headnorm(fk[0], gfk)
    fkn[0] = k1
    fkb[0] = k1.astype(BF16)
    fvb[0] = fv[0].astype(BF16)
    dqn[0] = headnorm(dq[0], gdq).astype(BF16)
    k2 = headnorm(dk[0], gdk)
    dkn[0] = k2
    dkb[0] = k2.astype(BF16)
    dvb[0] = dv[0].astype(BF16)
    lst = last[0]
    kid[0] = lst[:, 0:LANES].astype(BF16)
    small = lst[:, LANES:2 * LANES]
    logf = -_softplus(-(small + fb[...]))
    logf_o[0] = logf
    if do_cum:
        tm = small.shape[0]

        @pl.when(m == 0)
        def _():
            carry[...] = jnp.zeros_like(carry)

        lane = _iota(small.shape, 1)
        lf = jnp.where((lane >= SM_FF) & (lane < SM_FF + N_HEADS), logf, 0.0)
        cum = _dot3_l(_tri(tm, lambda r, c: r >= c), lf) + carry[0:1, :]
        carry[...] = jnp.broadcast_to(cum[tm - 1:tm, :], carry.shape)
        cumc[0] = cum
        cumr[0, 0] = cum.T[SM_FF:SM_FF + N_HEADS, :]


def _prep(u, wl, tm, do_cum):
    g, t, _ = u.shape
    tm = min(tm, t)
    nt = t // tm

    def ublk(k):
        return pl.BlockSpec((1, tm, BLK), lambda i, m: (i, m, k))

    def full(a):
        return pl.BlockSpec(a.shape, lambda i, m: (0,) * a.ndim)

    consts = [wl['gmat'], wl['fox_q_norm'], wl['fox_k_norm'], wl['dsa_q_norm'], wl['dsa_k_norm'], wl['fb_row']]
    in_specs = [ublk(BLK_FQ), ublk(BLK_FK), ublk(BLK_FV), ublk(BLK_DQ), ublk(BLK_DK), ublk(BLK_DV),
                ublk(BLK_LAST)] + [full(a) for a in consts]
    o512 = pl.BlockSpec((1, tm, BLK), lambda i, m: (i, m, 0))
    o128 = pl.BlockSpec((1, tm, LANES), lambda i, m: (i, m, 0))

    def s512(dt):
        return jax.ShapeDtypeStruct((g, t, BLK), dt)

    out_specs = [o512] * 8 + [o128, o128]
    out_shape = [s512(BF16), s512(F32), s512(BF16), s512(BF16), s512(BF16), s512(F32), s512(BF16), s512(BF16),
                 jax.ShapeDtypeStruct((g, t, LANES), BF16), jax.ShapeDtypeStruct((g, t, LANES), F32)]
    scratch = []
    if do_cum:
        out_specs += [o128, pl.BlockSpec((1, 1, N_HEADS, tm), lambda i, m: (i, m, 0, 0))]
        out_shape += [jax.ShapeDtypeStruct((g, t, LANES), F32),
                      jax.ShapeDtypeStruct((g, nt, N_HEADS, tm), F32)]
        scratch = [pltpu.VMEM((8, LANES), F32)]
    return pl.pallas_call(
        functools.partial(_prep_kernel, do_cum=do_cum), grid=(g, nt),
        in_specs=in_specs, out_specs=out_specs, out_shape=out_shape, scratch_shapes=scratch,
        compiler_params=_cp("arbitrary", "arbitrary"), name="qk_prep",
    )(*([u] * 7 + consts))


def _flash_scratch(tq):
    return [pltpu.VMEM((N_HEADS, tq, LANES), BF16), pltpu.VMEM((N_HEADS, tq, 1), F32),
            pltpu.VMEM((N_HEADS, tq, 1), F32), pltpu.VMEM((D_ATT // LANES, tq, LANES), F32)]


def _split_heads(q_ref, qm_scr):
    first = _iota((1, LANES), 1) < HEAD_DIM
    scale = jnp.asarray(HEAD_DIM ** -0.5, BF16)
    for j in range(D_ATT // LANES):
        qp = q_ref[0, :, j * LANES:(j + 1) * LANES].astype(BF16)
        zero = jnp.zeros_like(qp)
        qm_scr[2 * j] = jnp.where(first, qp, zero) * scale
        qm_scr[2 * j + 1] = jnp.where(first, zero, qp) * scale


def _flash_heads(q_ref, k_ref, v_ref, bias_of, nkb, tq, o_ref, qm_scr, m_scr, l_scr, acc_scr):
    first = _iota((1, LANES), 1) < HEAD_DIM
    npairs = D_ATT // LANES
    _split_heads(q_ref, qm_scr)
    m_scr[...] = jnp.full(m_scr.shape, NEG, F32)
    l_scr[...] = jnp.zeros_like(l_scr)
    acc_scr[...] = jnp.zeros_like(acc_scr)

    def body(kb, _):
        r0 = pl.multiple_of(kb * tq, tq)
        for j in range(npairs):
            sl = slice(j * LANES, (j + 1) * LANES)
            ks = k_ref[0, pl.ds(r0, tq), sl]
            vs = v_ref[0, pl.ds(r0, tq), sl]
            res = []
            for w in range(2):
                h = 2 * j + w
                s = _dot_nt(qm_scr[h], ks) + bias_of(kb, h)
                m_old = m_scr[h]
                m_new = jnp.maximum(m_old, jnp.max(s, axis=1, keepdims=True))
                p = jnp.exp(s - m_new)
                alpha = jnp.exp(m_old - m_new)
                l_scr[h] = alpha * l_scr[h] + jnp.sum(p, axis=1, keepdims=True)
                m_scr[h] = m_new
                res.append((alpha, _dot(p.astype(BF16), vs)))
            acc_scr[j] = (jnp.where(first, res[0][0], res[1][0]) * acc_scr[j]
                          + jnp.where(first, res[0][1], res[1][1]))
        return 0

    lax.fori_loop(0, nkb, body, 0)
    outs = []
    for j in range(npairs):
        outs.append(acc_scr[j] * jnp.where(first, 1.0 / l_scr[2 * j], 1.0 / l_scr[2 * j + 1]))
    o_ref[0] = jnp.concatenate(outs, axis=1).astype(BF16)


def _fox_prompt_kernel(q_ref, k_ref, v_ref, cc_ref, cr_ref, o_ref, qm_scr, m_scr, l_scr, acc_scr, *, tq):
    qi = pl.program_id(1)
    qpos = qi * tq + _iota((tq, 1), 0)

    def bias_of(kb, h):
        cq = cc_ref[0, :, SM_FF + h:SM_FF + h + 1]
        ck = cr_ref[0, kb][h:h + 1, :]
        kpos = kb * tq + _iota((1, tq), 1)
        return jnp.where(kpos <= qpos, cq - ck, NEG)

    _flash_heads(q_ref, k_ref, v_ref, bias_of, qi + 1, tq, o_ref, qm_scr, m_scr, l_scr, acc_scr)


def _fox_prompt(q, k, v, cumc, cumr, tq):
    b, t, _ = q.shape
    nt = t // tq
    return pl.pallas_call(
        functools.partial(_fox_prompt_kernel, tq=tq), grid=(b, nt),
        in_specs=[pl.BlockSpec((1, tq, D_ATT), lambda i, m: (i, m, 0)),
                  pl.BlockSpec((1, t, D_ATT), lambda i, m: (i, 0, 0)),
                  pl.BlockSpec((1, t, D_ATT), lambda i, m: (i, 0, 0)),
                  pl.BlockSpec((1, tq, LANES), lambda i, m: (i, m, 0)),
                  pl.BlockSpec((1, nt, N_HEADS, tq), lambda i, m: (i, 0, 0, 0))],
        out_specs=pl.BlockSpec((1, tq, D_ATT), lambda i, m: (i, m, 0)),
        out_shape=jax.ShapeDtypeStruct((b, t, D_ATT), BF16),
        scratch_shapes=_flash_scratch(tq),
        compiler_params=_cp("arbitrary", "arbitrary"), name="fox_prompt",
    )(q, k, v, cumc, cumr)


def _sort_key(score):
    bits = lax.bitcast_convert_type(score + 0.0, I32)
    return jnp.where(bits < 0, bits ^ jnp.int32(0x7FFFFFFF), bits)


def _kth_largest(count_ge, rows, k):
    imin = jnp.int32(-2 ** 31)

    def body(i, t):
        cand = t + lax.shift_left(jnp.int32(1), jnp.int32(31) - i)
        return jnp.where(count_ge(cand) >= k, cand, t)

    return lax.fori_loop(0, 32, body, jnp.full((rows, 1), imin, I32))


def _dsa_prompt_kernel(iq_ref, sm_ref, ki_ref, q_ref, k_ref, v_ref, o_ref,
                       key_scr, sel_scr, qm_scr, m_scr, l_scr, acc_scr, *, tq, topk):
    qi = pl.program_id(1)
    nkb = qi + 1
    qpos = qi * tq + _iota((tq, 1), 0)
    ncol = tq // LANES

    _split_heads(iq_ref, qm_scr)
    wsc = sm_ref[0] * (N_HEADS ** -0.5)

    def score_body(kb, _):
        ks = ki_ref[0, pl.ds(pl.multiple_of(kb * tq, tq), tq), :]
        acc = jnp.zeros((tq, tq), F32)
        for h in range(N_HEADS):
            d = jnp.maximum(_dot_nt(qm_scr[h], ks), 0.0)
            acc = acc + wsc[:, SM_IW + h:SM_IW + h + 1] * d
        kpos = kb * tq + _iota((1, tq), 1)
        key_scr[kb] = _sort_key(jnp.where(kpos <= qpos, acc, -jnp.inf))
        return 0

    lax.fori_loop(0, nkb, score_body, 0)

    def count(fn):
        def body(kb, part):
            blk = key_scr[kb]
            for c in range(ncol):
                part = part + jnp.where(fn(blk[:, c * LANES:(c + 1) * LANES]), 1.0, 0.0)
            return part

        part = lax.fori_loop(0, nkb, body, jnp.zeros((tq, LANES), F32))
        return jnp.sum(part, axis=1, keepdims=True)

    kf = float(topk)
    thr = _kth_largest(lambda cand: count(lambda x: x >= cand), tq, kf)
    need = kf - count(lambda x: x > thr)
    strict = _tri(tq, lambda r, c: r < c)

    def sel_body(kb, run):
        blk = key_scr[kb]
        eq = blk == thr
        eqf = jnp.where(eq, 1.0, 0.0)
        before = _dot(eqf.astype(BF16), strict) + run
        kpos = kb * tq + _iota((1, tq), 1)
        sel = ((blk > thr) | (eq & (before < need))) & (kpos <= qpos)
        sel_scr[kb] = jnp.where(sel, 0.0, NEG)
        return run + jnp.sum(eqf, axis=1, keepdims=True)

    lax.fori_loop(0, nkb, sel_body, jnp.zeros((tq, 1), F32))

    _flash_heads(q_ref, k_ref, v_ref, lambda kb, h: sel_scr[kb], nkb, tq, o_ref,
                 qm_scr, m_scr, l_scr, acc_scr)


def _dsa_prompt(u, kid, q, k, v, tq, topk):
    b, t, _ = q.shape
    nt = t // tq
    assert tq >= topk
    return pl.pallas_call(
        functools.partial(_dsa_prompt_kernel, tq=tq, topk=topk), grid=(b, nt),
        in_specs=[pl.BlockSpec((1, tq, BLK), lambda i, m: (i, m, BLK_IQ)),
                  pl.BlockSpec((1, tq, LANES), lambda i, m: (i, m, BLK_LAST * (BLK // LANES) + 1)),
                  pl.BlockSpec((1, t, LANES), lambda i, m: (i, 0, 0)),
                  pl.BlockSpec((1, tq, D_ATT), lambda i, m: (i, m, 0)),
                  pl.BlockSpec((1, t, D_ATT), lambda i, m: (i, 0, 0)),
                  pl.BlockSpec((1, t, D_ATT), lambda i, m: (i, 0, 0))],
        out_specs=pl.BlockSpec((1, tq, D_ATT), lambda i, m: (i, m, 0)),
        out_shape=jax.ShapeDtypeStruct((b, t, D_ATT), BF16),
        scratch_shapes=[pltpu.VMEM((nt, tq, tq), I32), pltpu.VMEM((nt, tq, tq), F32)] + _flash_scratch(tq),
        compiler_params=_cp("arbitrary", "arbitrary"), name="dsa_prompt",
    )(u, u, kid, q, k, v)


def _fox_bias_kernel(pt_ref, nlf_ref, *refs, pg):
    lf_refs = refs[:pg]
    col_o, r_o, run_scr = refs[pg:]
    s = pl.program_id(1)
    strict = _tri(LANES, lambda r, c: r > c)

    @pl.when(s == 0)
    def _():
        nlf = nlf_ref[0]
        r_o[0] = _dot3_r(nlf, strict)
        run_scr[...] = jnp.broadcast_to(jnp.sum(nlf, axis=1, keepdims=True), run_scr.shape)

    run = run_scr[...]
    for i in range(pg):
        lf = lf_refs[i][0, 0]
        col_o[0, pg - 1 - i] = -(_dot3_r(lf, strict) + run)
        run = run + jnp.sum(lf, axis=1, keepdims=True)
    run_scr[...] = run


def _fox_bias(pt, lfc_t, layer, nlf, pg):
    b, npg = pt.shape
    nst = npg // pg

    def page(i):
        return lambda bi, s, pt: (layer, pt[bi, npg - 1 - (s * pg + i)], 0, 0)

    return pl.pallas_call(
        functools.partial(_fox_bias_kernel, pg=pg),
        grid_spec=pltpu.PrefetchScalarGridSpec(
            num_scalar_prefetch=1, grid=(b, nst),
            in_specs=[pl.BlockSpec((1, N_HEADS, LANES), lambda bi, s, pt: (bi, 0, 0))]
            + [pl.BlockSpec((1, 1, N_HEADS, LANES), page(i)) for i in range(pg)],
            out_specs=[pl.BlockSpec((1, pg, N_HEADS, LANES), lambda bi, s, pt: (bi, nst - 1 - s, 0, 0)),
                       pl.BlockSpec((1, N_HEADS, LANES), lambda bi, s, pt: (bi, 0, 0))],
            scratch_shapes=[pltpu.VMEM((N_HEADS, LANES), F32)]),
        out_shape=[jax.ShapeDtypeStruct((b, npg, N_HEADS, LANES), F32),
                   jax.ShapeDtypeStruct((b, N_HEADS, LANES), F32)],
        compiler_params=_cp("arbitrary", "arbitrary"), name="fox_bias",
    )(pt, nlf, *([lfc_t] * pg))


def _index_scores(iq, w, ki, t_new):
    d = jnp.maximum(_dot_nt(iq, ki.astype(BF16)), 0.0) * w
    return jnp.sum(d.reshape(t_new, N_HEADS, d.shape[1]), axis=1)


def _idx_score_kernel(pt_ref, iq_ref, w_ref, *refs, pg, t_new):
    ki_refs = refs[:pg]
    o_ref = refs[pg]
    iq = (iq_ref[0] * (HEAD_DIM ** -0.5)).astype(BF16)
    w = w_ref[0] * (N_HEADS ** -0.5)
    pad = jnp.full((8 - t_new, LANES), -jnp.inf, F32)
    for i in range(pg):
        o_ref[0, i] = jnp.concatenate([_index_scores(iq, w, ki_refs[i][0, 0], t_new), pad], axis=0)


def _idx_scores(pt, kic, layer, iq, w, t_new, pg):
    b, npg = pt.shape
    rows = t_new * N_HEADS

    def page(i):
        return lambda bi, s, pt: (layer, pt[bi, s * pg + i], 0, 0)

    return pl.pallas_call(
        functools.partial(_idx_score_kernel, pg=pg, t_new=t_new),
        grid_spec=pltpu.PrefetchScalarGridSpec(
            num_scalar_prefetch=1, grid=(b, npg // pg),
            in_specs=[pl.BlockSpec((1, rows, HEAD_DIM), lambda bi, s, pt: (bi, 0, 0)),
                      pl.BlockSpec((1, rows, 1), lambda bi, s, pt: (bi, 0, 0))]
            + [pl.BlockSpec((1, 1, LANES, HEAD_DIM), page(i)) for i in range(pg)],
            out_specs=pl.BlockSpec((1, pg, 8, LANES), lambda bi, s, pt: (bi, s, 0, 0))),
        out_shape=jax.ShapeDtypeStruct((b, npg, 8, LANES), F32),
        compiler_params=_cp("arbitrary", "arbitrary"), name="dsa_idx_scores",
    )(pt, iq, w, *([kic] * pg))


def _select_kernel(sc_ref, iq_ref, w_ref, kin_ref, mf_o, mn_o, key_scr, *, t_new, topk):
    npg = sc_ref.shape[1]
    flat = LANES * N_HEADS
    key_scr[0:npg] = _sort_key(sc_ref[0])
    iq = (iq_ref[0] * (HEAD_DIM ** -0.5)).astype(BF16)
    w = w_ref[0] * (N_HEADS ** -0.5)
    scn = _index_scores(iq, w, kin_ref[0], t_new)
    scn = jnp.where(_iota(scn.shape, 1) <= _iota(scn.shape, 0), scn, -jnp.inf)
    pad = jnp.full((8 - t_new, LANES), -jnp.inf, F32)
    key_scr[npg] = _sort_key(jnp.concatenate([scn, pad], axis=0))

    def count(fn):
        part = jnp.sum(jnp.where(fn(key_scr[...]), 1.0, 0.0), axis=0)
        return jnp.sum(part, axis=1, keepdims=True)

    kf = float(topk)
    thr = _kth_largest(lambda cand: count(lambda k: k >= cand), 8, kf)
    need = kf - count(lambda k: k > thr)
    strict = _tri(LANES, lambda r, c: r < c)
    expand = jnp.where(lax.shift_right_logical(_iota((LANES, flat), 1), 3) == _iota((LANES, flat), 0),
                       1.0, 0.0).astype(BF16)
    minkey = _sort_key(jnp.full((8, LANES), -jnp.inf, F32))

    def members(blk, run):
        eq = blk == thr
        eqf = jnp.where(eq, 1.0, 0.0)
        before = _dot(eqf.astype(BF16), strict) + run
        sel = ((blk > thr) | (eq & (before < need))) & (blk > minkey)
        return _dot(jnp.where(sel, 1.0, 0.0).astype(BF16), expand), run + jnp.sum(eqf, axis=1, keepdims=True)

    def body(p, run):
        me, run = members(key_scr[p], run)
        mf_o[0, p] = me
        return run

    run = lax.fori_loop(0, npg, body, jnp.zeros((8, 1), F32))
    me, _ = members(key_scr[npg], run)
    mn_o[0] = me[:, 0:LANES]


def _dsa_select(scores, iq, w, ki_new, t_new, topk):
    b, npg, _, _ = scores.shape
    rows = t_new * N_HEADS
    flat = LANES * N_HEADS
    return pl.pallas_call(
        functools.partial(_select_kernel, t_new=t_new, topk=topk), grid=(b,),
        in_specs=[pl.BlockSpec((1, npg, 8, LANES), lambda i: (i, 0, 0, 0)),
                  pl.BlockSpec((1, rows, HEAD_DIM), lambda i: (i, 0, 0)),
                  pl.BlockSpec((1, rows, 1), lambda i: (i, 0, 0)),
                  pl.BlockSpec((1, LANES, HEAD_DIM), lambda i: (i, 0, 0))],
        out_specs=[pl.BlockSpec((1, npg, 8, flat), lambda i: (i, 0, 0, 0)),
                   pl.BlockSpec((1, 8, LANES), lambda i: (i, 0, 0))],
        out_shape=[jax.ShapeDtypeStruct((b, npg, 8, flat), F32),
                   jax.ShapeDtypeStruct((b, 8, LANES), F32)],
        scratch_shapes=[pltpu.VMEM((npg + 1, 8, LANES), I32)],
        compiler_params=_cp("arbitrary"), name="dsa_select",
    )(scores, iq, w, ki_new)


def _decode_kernel(pt_ref, q_ref, kn_ref, vn_ref, *refs, pg, nsteps, t_new, fox):
    k_refs = refs[:pg]
    v_refs = refs[pg:2 * pg]
    if fox:
        row_ref, col_ref, coln_ref, o_ref, m_scr, l_scr, acc_scr = refs[2 * pg:]
    else:
        msk_ref, mskn_ref, o_ref, m_scr, l_scr, acc_scr = refs[2 * pg:]
    s = pl.program_id(1)
    rows = t_new * N_HEADS
    flat = LANES * N_HEADS
    q = q_ref[0] * jnp.asarray(HEAD_DIM ** -0.5, BF16)

    def same_head(n):
        return (_iota((rows, n), 1) & 7) == (_iota((rows, n), 0) & 7)

    def per_query(me, n):
        return jnp.concatenate([jnp.broadcast_to(me[t:t + 1, :], (N_HEADS, n)) for t in range(t_new)], axis=0)

    @pl.when(s == 0)
    def _():
        sn = _dot_nt(q, kn_ref[0])
        if fox:
            c = _iota((rows, LANES), 1)
            r = _iota((rows, LANES), 0)
            ok = same_head(LANES) & (lax.shift_right_logical(c, 3) <= lax.shift_right_logical(r, 3)) & (c < rows)
            sn = sn + jnp.where(ok, row_ref[0] - coln_ref[0], NEG)
        else:
            sn = sn + jnp.where(same_head(LANES) & (per_query(mskn_ref[0], LANES) > 0.5), 0.0, NEG)
        m = jnp.max(sn, axis=1, keepdims=True)
        p = jnp.exp(sn - m)
        m_scr[...] = m
        l_scr[...] = jnp.sum(p, axis=1, keepdims=True)
        acc_scr[...] = _dot(p.astype(BF16), vn_ref[0])

    hm = same_head(flat)
    ss = []
    for i in range(pg):
        if fox:
            bias = jnp.where(hm, row_ref[0] - col_ref[0, i], NEG)
        else:
            bias = jnp.where(hm & (per_query(msk_ref[0, i], flat) > 0.5), 0.0, NEG)
        ss.append(_dot_nt(q, k_refs[i][0, 0].astype(BF16)) + bias)
    m_old = m_scr[...]
    m_new = m_old
    for x in ss:
        m_new = jnp.maximum(m_new, jnp.max(x, axis=1, keepdims=True))
    alpha = jnp.exp(m_old - m_new)
    l = alpha * l_scr[...]
    acc = alpha * acc_scr[...]
    for i, x in enumerate(ss):
        p = jnp.exp(x - m_new)
        l = l + jnp.sum(p, axis=1, keepdims=True)
        acc = acc + _dot(p.astype(BF16), v_refs[i][0, 0].astype(BF16))
    m_scr[...] = m_new
    l_scr[...] = l
    acc_scr[...] = acc

    @pl.when(s == nsteps - 1)
    def _():
        o_ref[0] = (acc / l).astype(BF16)


def _decode_attn(pt, q, kc, vc, layer, k_new, v_new, bias_args, t_new, pg, fox):
    b, npg = pt.shape
    rows = t_new * N_HEADS
    flat = LANES * N_HEADS
    nst = npg // pg

    def page(i):
        return lambda bi, s, pt: (layer, pt[bi, s * pg + i], 0, 0)

    def per_b(shape):
        return pl.BlockSpec((1,) + shape, lambda bi, s, pt: (bi,) + (0,) * len(shape))

    cache_specs = [pl.BlockSpec((1, 1, flat, HEAD_DIM), page(i)) for i in range(pg)]
    if fox:
        bias_specs = [per_b((rows, 1)),
                      pl.BlockSpec((1, pg, 1, flat), lambda bi, s, pt: (bi, s, 0, 0)),
                      per_b((1, LANES))]
    else:
        bias_specs = [pl.BlockSpec((1, pg, 8, flat), lambda bi, s, pt: (bi, s, 0, 0)),
                      per_b((8, LANES))]
    return pl.pallas_call(
        functools.partial(_decode_kernel, pg=pg, nsteps=nst, t_new=t_new, fox=fox),
        grid_spec=pltpu.PrefetchScalarGridSpec(
            num_scalar_prefetch=1, grid=(b, nst),
            in_specs=[per_b((rows, HEAD_DIM)), per_b((LANES, HEAD_DIM)), per_b((LANES, HEAD_DIM))]
            + cache_specs + cache_specs + bias_specs,
            out_specs=per_b((rows, HEAD_DIM)),
            scratch_shapes=[pltpu.VMEM((rows, 1), F32), pltpu.VMEM((rows, 1), F32),
                            pltpu.VMEM((rows, HEAD_DIM), F32)]),
        out_shape=jax.ShapeDtypeStruct((b, rows, HEAD_DIM), BF16),
        compiler_params=_cp("arbitrary", "arbitrary"), name="decode_attn",
    )(pt, q, k_new, v_new, *([kc] * pg), *([vc] * pg), *bias_args)


def _pack_w_in(w_in):
    o = 0
    seg = {}
    for name, n in (('z', 1024), ('xbc', 1536), ('dt', 16), ('fq', 512), ('fk', 512), ('fv', 512), ('ff', 8),
                    ('dq', 512), ('dk', 512), ('dv', 512), ('iq', 512), ('ik', 64), ('iw', 8)):
        seg[name] = w_in[:, :, o:o + n]
        o += n
    used = 1024 + 1536 + 7 * 512 + 64 + 64 + 16 + 8 + 8
    pad = jnp.zeros(w_in.shape[:2] + (N_PACK - used,), w_in.dtype)
    parts = [seg[k] for k in ('z', 'xbc', 'fq', 'fk', 'fv', 'dq', 'dk', 'dv', 'iq', 'ik', 'ik', 'dt', 'ff', 'iw')]
    return jnp.concatenate(parts + [pad], axis=-1).astype(BF16)


def _lane_row(vals, offset):
    return jnp.zeros((1, LANES), F32).at[0, offset:offset + vals.shape[0]].set(vals.astype(F32))


def _layer_consts(l, P):
    a = -jnp.exp(P['ssd_a_log'][l].astype(F32))
    dtb_row = _lane_row(P['ssd_dt_bias'][l], SM_DT)
    a_row = _lane_row(a, SM_DT)
    gmat = jnp.kron(jnp.eye(N_HEADS, dtype=F32), jnp.full((HEAD_DIM, HEAD_DIM), 1.0 / HEAD_DIM, F32)).astype(BF16)

    def tile_heads(g):
        return jnp.tile(g.astype(F32), N_HEADS).reshape(1, D_ATT)

    return dict(
        ssd_conv_w=P['ssd_conv_w'][l], ssd_conv_b=P['ssd_conv_b'][l].reshape(1, -1),
        dtb_row=dtb_row, dtb_col=dtb_row.reshape(LANES, 1), a_row=a_row, a_col=a_row.reshape(LANES, 1),
        d_exp=jnp.repeat(P['ssd_d'][l].astype(F32), HEAD_DIM).reshape(1, D_SSD),
        ssd_norm=P['ssd_norm'][l].reshape(1, D_SSD), gmat=gmat,
        fox_q_norm=tile_heads(P['fox_q_norm'][l]), fox_k_norm=tile_heads(P['fox_k_norm'][l]),
        dsa_q_norm=tile_heads(P['dsa_q_norm'][l]), dsa_k_norm=tile_heads(P['dsa_k_norm'][l]),
        fb_row=_lane_row(P['fox_f_bias'][l], SM_FF),
    )


def _pad_rows(x, rows):
    return jnp.pad(x, ((0, 0), (0, rows - x.shape[1]), (0, 0)))


def _layer(l, x, mod, P, wl, past):
    b, t, d = x.shape
    sh1, sc1, g1, sh2, sc2, g2 = [mod[:, None, i * d:(i + 1) * d] for i in range(6)]
    prompt = past is None
    m = b * t
    f2 = P['w_up'].shape[-1]

    h = _norm_mod(x, P['attn_norm'][l], sc1, sh1)
    if prompt:
        u = _matmul(h, P['w_in_p'][l], 512, BLK)
        uf = u
    else:
        uf = _matmul(h.reshape(1, m, d), P['w_in_p'][l], m, BLK)
        u = uf.reshape(b, t, N_PACK)

    if prompt:
        y_ssd, st = _ssd(u, wl, SSD_CHUNK)
    else:
        prev = jnp.pad(past['state_ssd_conv'][l], ((0, 0), (8 - (SSD_CONV - 1), 0), (0, 0)))
        h0 = past['state_ssd'][l].reshape(b, SSD_NHEADS * HEAD_DIM, SSD_STATE)
        y_ssd, st = _ssd(_pad_rows(u, SSD_CHUNK), wl, t, prev, h0)
        y_ssd = y_ssd[:, :t]
    ssd_new = st.reshape(b, SSD_NHEADS, HEAD_DIM, SSD_STATE)
    xbc_raw = u[:, :, BLK_Z * BLK + D_SSD:BLK_Z * BLK + D_SSD + 3 * BLK]
    if prompt:
        ssd_conv_new = xbc_raw[:, t - (SSD_CONV - 1):]
    else:
        ssd_conv_new = jnp.concatenate([past['state_ssd_conv'][l], xbc_raw], axis=1)[:, t:]

    tq = 256
    (fqn, fkn, fkb, fvb, dqn, dkn, dkb, dvb, kid, logf, *cum) = _prep(uf, wl, tq, prompt)
    fk_out = fkn.reshape(b, t, N_HEADS, HEAD_DIM)
    dk_out = dkn.reshape(b, t, N_HEADS, HEAD_DIM)
    fv_out = u[:, :, BLK_FV * BLK:(BLK_FV + 1) * BLK].reshape(b, t, N_HEADS, HEAD_DIM)
    dv_out = u[:, :, BLK_DV * BLK:(BLK_DV + 1) * BLK].reshape(b, t, N_HEADS, HEAD_DIM)
    ik_out = u[:, :, BLK_LAST * BLK:BLK_LAST * BLK + HEAD_DIM]
    logf_out = logf.reshape(b, t, LANES)[:, :, SM_FF:SM_FF + N_HEADS]

    if prompt:
        cumc, cumr = cum
        y_fox = _fox_prompt(fqn, fkb, fvb, cumc, cumr, tq)
        y_dsa = _dsa_prompt(u, kid, dqn, dkb, dvb, tq, min(TOPK_MAX, t // 4))
    else:
        pt = past['page_table']
        npg = pt.shape[1]
        page = past['cache_dsa_kidx'].shape[2]
        assert page == LANES
        rows = t * N_HEADS
        pg_kv = min(PAGES_PER_STEP_KV, npg)
        pg_aux = min(PAGES_PER_STEP_AUX, npg)

        def heads_flat(a):
            return a.reshape(b, rows, HEAD_DIM)

        def new_page(a):
            return _pad_rows(heads_flat(a), LANES)

        nlf = jnp.pad(jnp.swapaxes(logf_out, 1, 2), ((0, 0), (0, 0), (0, LANES - t)))
        col, rnew = _fox_bias(pt, past['fox_logf_t'], l, nlf, pg_aux)
        col_flat = jnp.swapaxes(col, 2, 3).reshape(b, npg, 1, LANES * N_HEADS)
        row_term = jnp.swapaxes(-rnew[:, :, :t], 1, 2).reshape(b, rows, 1)
        coln = jnp.pad(row_term.reshape(b, 1, rows), ((0, 0), (0, 0), (0, LANES - rows)))
        y_fox = _decode_attn(pt, heads_flat(fqn), past['fox_kf'], past['fox_vf'], l, new_page(fkb), new_page(fvb),
                             (row_term, col_flat, coln), t, pg_kv, True)

        iq = u[:, :, BLK_IQ * BLK:(BLK_IQ + 1) * BLK].reshape(b, rows, HEAD_DIM)
        iw = u[:, :, BLK_LAST * BLK + LANES + SM_IW:BLK_LAST * BLK + LANES + SM_IW + N_HEADS].reshape(b, rows, 1)
        sc = _idx_scores(pt, past['cache_dsa_kidx'], l, iq, iw, t, pg_aux)
        memb, memb_new = _dsa_select(sc, iq, iw, _pad_rows(ik_out, LANES), t,
                                     min(TOPK_MAX, (npg * page + t) // 4))
        y_dsa = _decode_attn(pt, heads_flat(dqn), past['dsa_kf'], past['dsa_vf'], l, new_page(dkb), new_page(dvb),
                             (memb, memb_new), t, pg_kv, False)

    mixed = jnp.concatenate([y_ssd, y_fox.reshape(b, t, D_ATT), y_dsa.reshape(b, t, D_ATT)], axis=-1)
    if prompt:
        x = _matmul(mixed, P['w_out_b'][l], 512, 512, res=x, gate=g1)
    else:
        x = _matmul(mixed.reshape(1, m, d), P['w_out_b'][l], m, 512, res=x.reshape(1, m, d),
                    gate=jnp.broadcast_to(g1, (b, t, d)).reshape(1, m, d)).reshape(b, t, d)

    h2 = _norm_mod(x, P['ffn_norm'][l], sc2, sh2)
    cw = P['ffn_conv_w'][l]
    cb = P['ffn_conv_b'][l].reshape(1, f2)
    if prompt:
        act, ta, tg = _ffn_up_prompt(h2.reshape(m, d), P['w_up_b'][l], cw, cb, t)
        ffn_conv_new = jnp.concatenate([ta, tg], axis=-1)[:, 8 - (FFN_CONV - 1):]
        x = _matmul(act.reshape(b, t, -1), P['w_down_b'][l], 512, 512, res=x, gate=g2)
    else:
        buf = past['state_ffn_conv'][l]
        zero = jnp.zeros((b, t - 1, f2), F32)
        p1 = jnp.concatenate([buf[:, 1:2], zero], axis=1).reshape(m, f2)
        p2 = jnp.concatenate([buf[:, 0:2], zero[:, 1:]], axis=1).reshape(m, f2)
        act, ra, rg = _ffn_up_sample(h2.reshape(m, d), P['w_up_b'][l], cw, cb, p1, p2, t)
        raw = jnp.concatenate([ra, rg], axis=-1).reshape(b, t, f2)
        ffn_conv_new = jnp.concatenate([buf, raw], axis=1)[:, t:]
        x = _matmul(act.reshape(1, m, -1), P['w_down_b'][l], m, 512, res=x.reshape(1, m, d),
                    gate=jnp.broadcast_to(g2, (b, t, d)).reshape(1, m, d)).reshape(b, t, d)

    new = (fk_out, fv_out, logf_out, dk_out, dv_out, ik_out, ssd_new, ssd_conv_new, ffn_conv_new)
    return x, new


def kernel(x_prompt, x_sample, cache_fox_k, cache_fox_v, cache_fox_logf, cache_dsa_k, cache_dsa_v, cache_dsa_kidx, state_ssd, state_ssd_conv, state_ffn_conv, page_table, c_prompt, c_sample, w_ada, b_ada, attn_norm, w_in, ssd_conv_w, ssd_conv_b, ssd_dt_bias, ssd_a_log, ssd_d, ssd_norm, fox_q_norm, fox_k_norm, fox_f_bias, dsa_q_norm, dsa_k_norm, w_out, ffn_norm, w_up, ffn_conv_w, ffn_conv_b, w_down):
    depth = w_in.shape[0]
    bp = x_prompt.shape[0]
    bs = x_sample.shape[0]
    P = dict(attn_norm=attn_norm, ssd_conv_w=ssd_conv_w, ssd_conv_b=ssd_conv_b, ssd_dt_bias=ssd_dt_bias,
             ssd_a_log=ssd_a_log, ssd_d=ssd_d, ssd_norm=ssd_norm, fox_q_norm=fox_q_norm, fox_k_norm=fox_k_norm,
             fox_f_bias=fox_f_bias, dsa_q_norm=dsa_q_norm, dsa_k_norm=dsa_k_norm, ffn_norm=ffn_norm,
             ffn_conv_w=ffn_conv_w, ffn_conv_b=ffn_conv_b, w_up=w_up,
             w_in_p=_pack_w_in(w_in), w_out_b=w_out.astype(BF16), w_up_b=w_up.astype(BF16),
             w_down_b=w_down.astype(BF16))
    npool, page = cache_fox_k.shape[1], cache_fox_k.shape[2]

    def keys_heads_flat(c):
        return c.reshape(depth, npool, page * N_HEADS, HEAD_DIM)

    past = dict(state_ssd=state_ssd, state_ssd_conv=state_ssd_conv, state_ffn_conv=state_ffn_conv,
                page_table=page_table, cache_dsa_kidx=cache_dsa_kidx,
                fox_kf=keys_heads_flat(cache_fox_k), fox_vf=keys_heads_flat(cache_fox_v),
                dsa_kf=keys_heads_flat(cache_dsa_k), dsa_vf=keys_heads_flat(cache_dsa_v),
                fox_logf_t=jnp.swapaxes(cache_fox_logf, 2, 3))

    rows = bp + bs
    rows_pad = -(-rows // 8) * 8
    c_all = jnp.pad(jnp.concatenate([c_prompt, c_sample], axis=0), ((0, rows_pad - rows), (0, 0)))
    mods = _ada_mod(c_all, w_ada, b_ada)

    xp, xs = x_prompt, x_sample
    new_p, new_s = [], []
    for l in range(depth):
        wl = _layer_consts(l, P)
        xp, sp = _layer(l, xp, mods[l, :bp], P, wl, None)
        xs, ss = _layer(l, xs, mods[l, bp:rows], P, wl, past)
        new_p.append(sp)
        new_s.append(ss)
    outs_p = [jnp.stack(a, 0) for a in zip(*new_p)]
    outs_s = [jnp.stack(a, 0) for a in zip(*new_s)]
    return tuple([xp, xs] + outs_p + outs_s)
```

```python
import functools

import jax
import jax.numpy as jnp
from jax import lax
from jax.experimental import pallas as pl
from jax.experimental.pallas import tpu as pltpu

F32 = jnp.float32
BF16 = jnp.bfloat16
I32 = jnp.int32

RMS_EPS = 1e-6
NEG = -1e30
HEAD_DIM = 64
LANES = 128
D_SSD = 1024
SSD_STATE = 128
SSD_NHEADS = 16
SSD_CONV = 4
SSD_CHUNK = 128
D_ATT = 512
N_HEADS = 8
FFN_CONV = 3
TOPK_MAX = 256
VMEM_LIMIT = 56 * 1024 * 1024
PAGES_PER_STEP_KV = 8
PAGES_PER_STEP_AUX = 16

BLK = 512
BLK_Z, BLK_X, BLK_FQ, BLK_FK, BLK_FV, BLK_DQ, BLK_DK, BLK_DV, BLK_IQ, BLK_LAST = 0, 2, 5, 6, 7, 8, 9, 10, 11, 12
N_PACK = 13 * BLK
SM_DT, SM_FF, SM_IW = 0, 16, 24


def _cp(*sem):
    return pltpu.CompilerParams(dimension_semantics=sem, vmem_limit_bytes=VMEM_LIMIT)


def _dot(a, b):
    return jnp.dot(a, b, preferred_element_type=F32)


def _dot_nt(a, b):
    return lax.dot_general(a, b, (((1,), (1,)), ((), ())), preferred_element_type=F32)


def _split3(x):
    hi = x.astype(BF16)
    r = x - hi.astype(F32)
    mid = r.astype(BF16)
    lo = (r - mid.astype(F32)).astype(BF16)
    return hi, mid, lo


def _dot3_l(m01, x):
    hi, mid, lo = _split3(x)
    return _dot(m01, hi) + _dot(m01, mid) + _dot(m01, lo)


def _dot3_r(x, m01):
    hi, mid, lo = _split3(x)
    return _dot(hi, m01) + _dot(mid, m01) + _dot(lo, m01)


def _softplus(x):
    return jnp.maximum(x, 0.0) + jnp.log1p(jnp.exp(-jnp.abs(x)))


def _silu(x):
    return x * jax.nn.sigmoid(x)


def _iota(shape, dim):
    return lax.broadcasted_iota(I32, shape, dim)


def _tri(n, fn):
    r = _iota((n, n), 0)
    c = _iota((n, n), 1)
    return jnp.where(fn(r, c), 1.0, 0.0).astype(BF16)


def _ada_kernel(c_ref, w_ref, b_ref, o_ref):
    s = _silu(c_ref[...]).astype(BF16)
    o_ref[0] = _dot(s, w_ref[0].astype(BF16)) + b_ref[0]


def _ada_mod(c_all, w_ada, b_ada):
    nl, d, n = w_ada.shape
    rows = c_all.shape[0]
    tn = 1024
    return pl.pallas_call(
        _ada_kernel, grid=(nl, n // tn),
        in_specs=[pl.BlockSpec((rows, d), lambda l, j: (0, 0)),
                  pl.BlockSpec((1, d, tn), lambda l, j: (l, 0, j)),
                  pl.BlockSpec((1, 1, tn), lambda l, j: (l, 0, j))],
        out_specs=pl.BlockSpec((1, rows, tn), lambda l, j: (l, 0, j)),
        out_shape=jax.ShapeDtypeStruct((nl, rows, n), F32),
        compiler_params=_cp("arbitrary", "arbitrary"), name="ada_mod",
    )(c_all, w_ada, b_ada.reshape(nl, 1, n))


def _normmod_kernel(x_ref, g_ref, sc_ref, sh_ref, o_ref):
    x = x_ref[0]
    ms = jnp.mean(x * x, axis=-1, keepdims=True)
    y = x * lax.rsqrt(ms + RMS_EPS) * g_ref[...]
    o_ref[0] = (y * (1.0 + sc_ref[0]) + sh_ref[0]).astype(BF16)


def _norm_mod(x, g, sc, sh):
    b, t, d = x.shape
    tm = min(t, 512)
    return pl.pallas_call(
        _normmod_kernel, grid=(b, t // tm),
        in_specs=[pl.BlockSpec((1, tm, d), lambda i, m: (i, m, 0)),
                  pl.BlockSpec((1, d), lambda i, m: (0, 0)),
                  pl.BlockSpec((1, 1, d), lambda i, m: (i, 0, 0)),
                  pl.BlockSpec((1, 1, d), lambda i, m: (i, 0, 0))],
        out_specs=pl.BlockSpec((1, tm, d), lambda i, m: (i, m, 0)),
        out_shape=jax.ShapeDtypeStruct((b, t, d), BF16),
        compiler_params=_cp("arbitrary", "arbitrary"), name="norm_mod",
    )(x, g.reshape(1, d), sc, sh)


def _mm_kernel(x_ref, w_ref, o_ref):
    o_ref[0] = _dot(x_ref[0], w_ref[...])


def _mm_res_kernel(x_ref, w_ref, r_ref, g_ref, o_ref):
    o_ref[0] = r_ref[0] + g_ref[0] * _dot(x_ref[0], w_ref[...])


def _matmul(x, w, tm, tn, res=None, gate=None):
    g, t, k = x.shape
    n = w.shape[1]
    tm = min(tm, t)
    grid = (n // tn, g, t // tm)
    x_spec = pl.BlockSpec((1, tm, k), lambda j, i, m: (i, m, 0))
    w_spec = pl.BlockSpec((k, tn), lambda j, i, m: (0, j))
    o_spec = pl.BlockSpec((1, tm, tn), lambda j, i, m: (i, m, j))
    out_shape = jax.ShapeDtypeStruct((g, t, n), F32)
    cp = _cp("arbitrary", "arbitrary", "arbitrary")
    if res is None:
        return pl.pallas_call(_mm_kernel, grid=grid, in_specs=[x_spec, w_spec], out_specs=o_spec,
                              out_shape=out_shape, compiler_params=cp, name="proj")(x, w)
    if gate.shape[1] == 1:
        g_spec = pl.BlockSpec((1, 1, tn), lambda j, i, m: (i, 0, j))
    else:
        g_spec = pl.BlockSpec((1, tm, tn), lambda j, i, m: (i, m, j))
    return pl.pallas_call(_mm_res_kernel, grid=grid, in_specs=[x_spec, w_spec, o_spec, g_spec],
                          out_specs=o_spec, out_shape=out_shape, compiler_params=cp,
                          name="proj_res")(x, w, res, gate)


def _conv3(u, prev1, prev2, cw_ref, cb_ref):
    return cb_ref[...] + prev2 * cw_ref[0:1, :] + prev1 * cw_ref[1:2, :] + u * cw_ref[2:3, :]


def _shift_with_carry(u, carry):
    r1 = pltpu.roll(u, 1, 0)
    r2 = pltpu.roll(u, 2, 0)
    c1 = pltpu.roll(carry, 1, 0)
    c2 = pltpu.roll(carry, 2, 0)
    row = _iota((8, u.shape[1]), 0)
    p1 = jnp.concatenate([jnp.where(row < 1, c1, r1[0:8]), r1[8:]], axis=0)
    p2 = jnp.concatenate([jnp.where(row < 2, c2, r2[0:8]), r2[8:]], axis=0)
    return p1, p2


def _ffn_up_prompt_kernel(x_ref, wa_ref, wg_ref, cwa_ref, cwg_ref, cba_ref, cbg_ref,
                          act_ref, ta_ref, tg_ref, ca_scr, cg_scr, *, tiles_per_seq):
    m = pl.program_id(1)
    tm = x_ref.shape[0]

    @pl.when(m % tiles_per_seq == 0)
    def _():
        ca_scr[...] = jnp.zeros_like(ca_scr)
        cg_scr[...] = jnp.zeros_like(cg_scr)

    x = x_ref[...]
    ua = _dot(x, wa_ref[...])
    ug = _dot(x, wg_ref[...])
    pa1, pa2 = _shift_with_carry(ua, ca_scr[...])
    pg1, pg2 = _shift_with_carry(ug, cg_scr[...])
    a = _conv3(ua, pa1, pa2, cwa_ref, cba_ref)
    gt = _conv3(ug, pg1, pg2, cwg_ref, cbg_ref)
    act_ref[...] = (_silu(gt) * a).astype(BF16)
    ca_scr[...] = ua[tm - 8:tm]
    cg_scr[...] = ug[tm - 8:tm]
    ta_ref[0] = ua[tm - 8:tm]
    tg_ref[0] = ug[tm - 8:tm]


def _ffn_up_prompt(h, w_up, cw, cb, t_seq):
    m, k = h.shape
    f = w_up.shape[1] // 2
    tm, tn = 512, 512
    nj = f // tn
    nb = m // t_seq
    tps = t_seq // tm
    cw_spec_a = pl.BlockSpec((FFN_CONV, tn), lambda j, i: (0, j))
    cw_spec_g = pl.BlockSpec((FFN_CONV, tn), lambda j, i: (0, j + nj))
    cb_spec_a = pl.BlockSpec((1, tn), lambda j, i: (0, j))
    cb_spec_g = pl.BlockSpec((1, tn), lambda j, i: (0, j + nj))
    tail_spec = pl.BlockSpec((1, 8, tn), lambda j, i: (i // tps, 0, j))
    return pl.pallas_call(
        functools.partial(_ffn_up_prompt_kernel, tiles_per_seq=tps),
        grid=(nj, m // tm),
        in_specs=[pl.BlockSpec((tm, k), lambda j, i: (i, 0)),
                  pl.BlockSpec((k, tn), lambda j, i: (0, j)),
                  pl.BlockSpec((k, tn), lambda j, i: (0, j + nj)),
                  cw_spec_a, cw_spec_g, cb_spec_a, cb_spec_g],
        out_specs=[pl.BlockSpec((tm, tn), lambda j, i: (i, j)), tail_spec, tail_spec],
        out_shape=[jax.ShapeDtypeStruct((m, f), BF16),
                   jax.ShapeDtypeStruct((nb, 8, f), F32),
                   jax.ShapeDtypeStruct((nb, 8, f), F32)],
        scratch_shapes=[pltpu.VMEM((8, tn), F32), pltpu.VMEM((8, tn), F32)],
        compiler_params=_cp("arbitrary", "arbitrary"), name="ffn_up_prompt",
    )(h, w_up, w_up, cw, cw, cb, cb)


def _ffn_up_sample_kernel(x_ref, wa_ref, wg_ref, cwa_ref, cwg_ref, cba_ref, cbg_ref,
                          p1a_ref, p2a_ref, p1g_ref, p2g_ref, act_ref, ra_ref, rg_ref, *, t_seq):
    x = x_ref[...]
    ua = _dot(x, wa_ref[...])
    ug = _dot(x, wg_ref[...])
    t = _iota(ua.shape, 0) % t_seq

    def prevs(u, p1_ref, p2_ref):
        p1 = jnp.where(t >= 1, pltpu.roll(u, 1, 0), 0.0) + p1_ref[...]
        p2 = jnp.where(t >= 2, pltpu.roll(u, 2, 0), 0.0) + p2_ref[...]
        return p1, p2

    pa1, pa2 = prevs(ua, p1a_ref, p2a_ref)
    pg1, pg2 = prevs(ug, p1g_ref, p2g_ref)
    a = _conv3(ua, pa1, pa2, cwa_ref, cba_ref)
    gt = _conv3(ug, pg1, pg2, cwg_ref, cbg_ref)
    act_ref[...] = (_silu(gt) * a).astype(BF16)
    ra_ref[...] = ua
    rg_ref[...] = ug


def _ffn_up_sample(h, w_up, cw, cb, p1, p2, t_seq):
    m, k = h.shape
    f = w_up.shape[1] // 2
    tn = 512
    nj = f // tn
    a_spec = pl.BlockSpec((m, tn), lambda j: (0, j))
    g_spec = pl.BlockSpec((m, tn), lambda j: (0, j + nj))
    return pl.pallas_call(
        functools.partial(_ffn_up_sample_kernel, t_seq=t_seq),
        grid=(nj,),
        in_specs=[pl.BlockSpec((m, k), lambda j: (0, 0)),
                  pl.BlockSpec((k, tn), lambda j: (0, j)),
                  pl.BlockSpec((k, tn), lambda j: (0, j + nj)),
                  pl.BlockSpec((FFN_CONV, tn), lambda j: (0, j)),
                  pl.BlockSpec((FFN_CONV, tn), lambda j: (0, j + nj)),
                  pl.BlockSpec((1, tn), lambda j: (0, j)),
                  pl.BlockSpec((1, tn), lambda j: (0, j + nj)),
                  a_spec, a_spec, g_spec, g_spec],
        out_specs=[a_spec, a_spec, a_spec],
        out_shape=[jax.ShapeDtypeStruct((m, f), BF16),
                   jax.ShapeDtypeStruct((m, f), F32),
                   jax.ShapeDtypeStruct((m, f), F32)],
        compiler_params=_cp("arbitrary"), name="ffn_up_sample",
    )(h, w_up, w_up, cw, cw, cb, cb, p1, p2, p1, p2)


def _ssd_kernel(*refs, valid, has_init, nchunks):
    if has_init:
        (z0, z1, x0, x1, x2, sm_ref, cw_ref, cb_ref, dtb_row, dtb_col, a_row, a_col, d_ref, ng_ref,
         prev_ref, h0_ref, y_ref, st_ref, xx_scr, st_scr) = refs
    else:
        (z0, z1, x0, x1, x2, sm_ref, cw_ref, cb_ref, dtb_row, dtb_col, a_row, a_col, d_ref, ng_ref,
         y_ref, st_ref, xx_scr, st_scr) = refs
    c = pl.program_id(1)
    L = SSD_CHUNK
    npair = SSD_NHEADS // 2

    @pl.when(c == 0)
    def _():
        if has_init:
            xx_scr[0:8, :] = prev_ref[0]
            for j in range(npair):
                st_scr[j] = h0_ref[0, j * L:(j + 1) * L, :].T
        else:
            xx_scr[0:8, :] = jnp.zeros((8, xx_scr.shape[1]), F32)
            st_scr[...] = jnp.zeros_like(st_scr)

    xx_scr[8:8 + L, 0:BLK] = x0[0]
    xx_scr[8:8 + L, BLK:2 * BLK] = x1[0]
    xx_scr[8:8 + L, 2 * BLK:3 * BLK] = x2[0]
    acc = cb_ref[...] + xx_scr[5:5 + L, :] * cw_ref[0:1, :]
    for i in range(1, SSD_CONV):
        acc = acc + xx_scr[5 + i:5 + i + L, :] * cw_ref[i:i + 1, :]
    xx_scr[0:8, :] = xx_scr[L:L + 8, :]
    xbc = _silu(acc)
    xs = xbc[:, 0:D_SSD]
    bm = xbc[:, D_SSD:D_SSD + 2 * SSD_STATE]
    cm = xbc[:, D_SSD + 2 * SSD_STATE:D_SSD + 4 * SSD_STATE]

    small = sm_ref[0]
    dt_c = _softplus(small + dtb_row[...])
    small_t = small.T
    dt_r = _softplus(small_t + dtb_col[...])
    if valid < L:
        dt_c = jnp.where(_iota((L, LANES), 0) < valid, dt_c, 0.0)
        dt_r = jnp.where(_iota((LANES, L), 1) < valid, dt_r, 0.0)
    a_c = dt_c * a_row[...]
    a_r = dt_r * a_col[...]
    tril = _tri(L, lambda r, cc: r >= cc)
    triu = _tri(L, lambda r, cc: r <= cc)
    ones = jnp.ones((L, L), BF16)
    cs_c = _dot3_l(tril, a_c)
    cs_r = _dot3_r(a_r, triu)
    tot_c = _dot3_l(ones, a_c)
    e_cs = jnp.exp(cs_c)
    e_dec = jnp.exp(tot_c - cs_c)
    e_tot = jnp.exp(tot_c)

    lane = _iota((L, LANES), 1)
    first = lane < HEAD_DIM
    causal = _iota((L, L), 0) >= _iota((L, L), 1)

    def bcast(mat, h):
        return jnp.broadcast_to(mat[:, h:h + 1], (L, LANES))

    def pair(mat, h):
        return jnp.where(first, bcast(mat, h), bcast(mat, h + 1))

    ys = []
    for g in range(2):
        b_g = bm[:, g * SSD_STATE:(g + 1) * SSD_STATE]
        c_g = cm[:, g * SSD_STATE:(g + 1) * SSD_STATE].astype(BF16)
        cb_g = _dot_nt(c_g, b_g.astype(BF16))
        bt_g = b_g.T.astype(BF16)
        for jj in range(npair // 2):
            j = g * (npair // 2) + jj
            ha = 2 * j
            xs_p = xs[:, j * LANES:(j + 1) * LANES]
            x_p = xs_p * pair(dt_c, ha)
            x_pb = x_p.astype(BF16)
            yd = []
            for h in (ha, ha + 1):
                diff = jnp.minimum(bcast(cs_c, h) - cs_r[h:h + 1, :], 0.0)
                lm = jnp.where(causal, jnp.exp(diff), 0.0)
                yd.append(_dot((cb_g * lm).astype(BF16), x_pb))
            y_diag = jnp.where(first, yd[0], yd[1])
            st_old = st_scr[j]
            y_off = _dot(c_g, st_old.astype(BF16)) * pair(e_cs, ha)
            ys.append(y_diag + y_off + d_ref[:, j * LANES:(j + 1) * LANES] * xs_p)
            xd = (x_p * pair(e_dec, ha)).astype(BF16)
            st_scr[j] = pair(e_tot, ha) * st_old + _dot(bt_g, xd)
    y = jnp.concatenate(ys, axis=1)

    zg = jnp.concatenate([z0[0], z1[0]], axis=1)
    y = y * _silu(zg)
    half = D_SSD // 2
    outs = []
    for g in range(2):
        yg = y[:, g * half:(g + 1) * half]
        ms = jnp.mean(yg * yg, axis=-1, keepdims=True)
        outs.append(yg * lax.rsqrt(ms + RMS_EPS) * ng_ref[:, g * half:(g + 1) * half])
    y_ref[0] = jnp.concatenate(outs, axis=1).astype(BF16)

    @pl.when(c == nchunks - 1)
    def _():
        for j in range(npair):
            st_ref[0, j * L:(j + 1) * L, :] = st_scr[j].T


def _ssd(u, wl, valid, prev=None, h0=None):
    b, t, _ = u.shape
    L = SSD_CHUNK
    nch = t // L
    has_init = prev is not None

    def ublk(k):
        return pl.BlockSpec((1, L, BLK), lambda i, c: (i, c, k))

    def full(a):
        return pl.BlockSpec(a.shape, lambda i, c: (0,) * a.ndim)

    consts = [wl['ssd_conv_w'], wl['ssd_conv_b'], wl['dtb_row'], wl['dtb_col'], wl['a_row'], wl['a_col'],
              wl['d_exp'], wl['ssd_norm']]
    in_specs = [ublk(BLK_Z), ublk(BLK_Z + 1), ublk(BLK_X), ublk(BLK_X + 1), ublk(BLK_X + 2),
                pl.BlockSpec((1, L, LANES), lambda i, c: (i, c, BLK_LAST * (BLK // LANES) + 1))]
    in_specs += [full(a) for a in consts]
    args = [u, u, u, u, u, u] + consts
    if has_init:
        in_specs += [pl.BlockSpec((1, 8, 3 * BLK), lambda i, c: (i, 0, 0)),
                     pl.BlockSpec((1, SSD_NHEADS * HEAD_DIM, SSD_STATE), lambda i, c: (i, 0, 0))]
        args += [prev, h0]
    return pl.pallas_call(
        functools.partial(_ssd_kernel, valid=valid, has_init=has_init, nchunks=nch),
        grid=(b, nch), in_specs=in_specs,
        out_specs=[pl.BlockSpec((1, L, D_SSD), lambda i, c: (i, c, 0)),
                   pl.BlockSpec((1, SSD_NHEADS * HEAD_DIM, SSD_STATE), lambda i, c: (i, 0, 0))],
        out_shape=[jax.ShapeDtypeStruct((b, t, D_SSD), BF16),
                   jax.ShapeDtypeStruct((b, SSD_NHEADS * HEAD_DIM, SSD_STATE), F32)],
        scratch_shapes=[pltpu.VMEM((L + 8, 3 * BLK), F32),
                        pltpu.VMEM((SSD_NHEADS // 2, SSD_STATE, LANES), F32)],
        compiler_params=_cp("arbitrary", "arbitrary"), name="ssd",
    )(*args)


def _prep_kernel(*refs, do_cum):
    (fq, fk, fv, dq, dk, dv, last, gm, gfq, gfk, gdq, gdk, fb) = refs[:13]
    if do_cum:
        (fqn, fkn, fkb, fvb, dqn, dkn, dkb, dvb, kid, logf_o, cumc, cumr, carry) = refs[13:]
    else:
        (fqn, fkn, fkb, fvb, dqn, dkn, dkb, dvb, kid, logf_o) = refs[13:]
    m = pl.program_id(1)
    gmat = gm[...]

    def headnorm(x, g_ref):
        sq = x * x
        hi = sq.astype(BF16)
        lo = (sq - hi.astype(F32)).astype(BF16)
        ms = _dot(hi, gmat) + _dot(lo, gmat)
        return x * lax.rsqrt(ms + RMS_EPS) * g_ref[...]

    fqn[0] = headnorm(fq[0], gfq).astype(BF16)
    k1 = headnorm(fk[0], gfk)
    fkn[0] = k1
    fkb[0] = k1.astype(BF16)
    fvb[0] = fv[0].astype(BF16)
    dqn[0] = headnorm(dq[0], gdq).astype(BF16)
    k2 = headnorm(dk[0], gdk)
    dkn[0] = k2
    dkb[0] = k2.astype(BF16)
    dvb[0] = dv[0].astype(BF16)
    lst = last[0]
    kid[0] = lst[:, 0:LANES].astype(BF16)
    small = lst[:, LANES:2 * LANES]
    logf = -_softplus(-(small + fb[...]))
    logf_o[0] = logf
    if do_cum:
        tm = small.shape[0]

        @pl.when(m == 0)
        def _():
            carry[...] = jnp.zeros_like(carry)

        lane = _iota(small.shape, 1)
        lf = jnp.where((lane >= SM_FF) & (lane < SM_FF + N_HEADS), logf, 0.0)
        cum = _dot3_l(_tri(tm, lambda r, c: r >= c), lf) + carry[0:1, :]
        carry[...] = jnp.broadcast_to(cum[tm - 1:tm, :], carry.shape)
        cumc[0] = cum
        cumr[0, 0] = cum.T[SM_FF:SM_FF + N_HEADS, :]


def _prep(u, wl, tm, do_cum):
    g, t, _ = u.shape
    tm = min(tm, t)
    nt = t // tm

    def ublk(k):
        return pl.BlockSpec((1, tm, BLK), lambda i, m: (i, m, k))

    def full(a):
        return pl.BlockSpec(a.shape, lambda i, m: (0,) * a.ndim)

    consts = [wl['gmat'], wl['fox_q_norm'], wl['fox_k_norm'], wl['dsa_q_norm'], wl['dsa_k_norm'], wl['fb_row']]
    in_specs = [ublk(BLK_FQ), ublk(BLK_FK), ublk(BLK_FV), ublk(BLK_DQ), ublk(BLK_DK), ublk(BLK_DV),
                ublk(BLK_LAST)] + [full(a) for a in consts]
    o512 = pl.BlockSpec((1, tm, BLK), lambda i, m: (i, m, 0))
    o128 = pl.BlockSpec((1, tm, LANES), lambda i, m: (i, m, 0))

    def s512(dt):
        return jax.ShapeDtypeStruct((g, t, BLK), dt)

    out_specs = [o512] * 8 + [o128, o128]
    out_shape = [s512(BF16), s512(F32), s512(BF16), s512(BF16), s512(BF16), s512(F32), s512(BF16), s512(BF16),
                 jax.ShapeDtypeStruct((g, t, LANES), BF16), jax.ShapeDtypeStruct((g, t, LANES), F32)]
    scratch = []
    if do_cum:
        out_specs += [o128, pl.BlockSpec((1, 1, N_HEADS, tm), lambda i, m: (i, m, 0, 0))]
        out_shape += [jax.ShapeDtypeStruct((g, t, LANES), F32),
                      jax.ShapeDtypeStruct((g, nt, N_HEADS, tm), F32)]
        scratch = [pltpu.VMEM((8, LANES), F32)]
    return pl.pallas_call(
        functools.partial(_prep_kernel, do_cum=do_cum), grid=(g, nt),
        in_specs=in_specs, out_specs=out_specs, out_shape=out_shape, scratch_shapes=scratch,
        compiler_params=_cp("arbitrary", "arbitrary"), name="qk_prep",
    )(*([u] * 7 + consts))


def _split_pair(q_pair):
    first = _iota((1, LANES), 1) < HEAD_DIM
    zero = jnp.zeros_like(q_pair)
    scale = jnp.asarray(HEAD_DIM ** -0.5, BF16)
    return jnp.where(first, q_pair, zero) * scale, jnp.where(first, zero, q_pair) * scale


def _flash_pair(q_pair, k_of, v_of, bias_of, nkb, tq):
    first = _iota((1, LANES), 1) < HEAD_DIM
    qa, qb = _split_pair(q_pair)

    def body(kb, carry):
        ma, la, mb, lb, acc = carry
        ks = k_of(kb)
        vs = v_of(kb)

        def half(qh, m_old, l_old, which):
            s = _dot_nt(qh, ks) + bias_of(kb, which)
            m_new = jnp.maximum(m_old, jnp.max(s, axis=1, keepdims=True))
            p = jnp.exp(s - m_new)
            alpha = jnp.exp(m_old - m_new)
            l_new = alpha * l_old + jnp.sum(p, axis=1, keepdims=True)
            return m_new, l_new, alpha, _dot(p.astype(BF16), vs)

        ma, la, aa, oa = half(qa, ma, la, 0)
        mb, lb, ab, ob = half(qb, mb, lb, 1)
        acc = jnp.where(first, aa, ab) * acc + jnp.where(first, oa, ob)
        return ma, la, mb, lb, acc

    m0 = jnp.full((tq, 1), NEG, F32)
    l0 = jnp.zeros((tq, 1), F32)
    ma, la, mb, lb, acc = lax.fori_loop(0, nkb, body, (m0, l0, m0, l0, jnp.zeros((tq, LANES), F32)))
    return acc * jnp.where(first, 1.0 / la, 1.0 / lb)


def _fox_prompt_kernel(q_ref, k_ref, v_ref, cc_ref, cr_ref, o_ref, *, tq):
    qi = pl.program_id(1)
    nkb = qi + 1
    qpos = qi * tq + _iota((tq, 1), 0)
    outs = []
    for j in range(D_ATT // LANES):
        sl = slice(j * LANES, (j + 1) * LANES)

        def k_of(kb, sl=sl):
            return k_ref[0, pl.ds(pl.multiple_of(kb * tq, tq), tq), sl]

        def v_of(kb, sl=sl):
            return v_ref[0, pl.ds(pl.multiple_of(kb * tq, tq), tq), sl]

        def bias_of(kb, which, j=j):
            h = 2 * j + which
            cq = cc_ref[0, :, SM_FF + h:SM_FF + h + 1]
            ck = cr_ref[0, kb][h:h + 1, :]
            kpos = kb * tq + _iota((1, tq), 1)
            return jnp.where(kpos <= qpos, cq - ck, NEG)

        outs.append(_flash_pair(q_ref[0, :, sl], k_of, v_of, bias_of, nkb, tq))
    o_ref[0] = jnp.concatenate(outs, axis=1).astype(BF16)


def _fox_prompt(q, k, v, cumc, cumr, tq):
    b, t, _ = q.shape
    nt = t // tq
    return pl.pallas_call(
        functools.partial(_fox_prompt_kernel, tq=tq), grid=(b, nt),
        in_specs=[pl.BlockSpec((1, tq, D_ATT), lambda i, m: (i, m, 0)),
                  pl.BlockSpec((1, t, D_ATT), lambda i, m: (i, 0, 0)),
                  pl.BlockSpec((1, t, D_ATT), lambda i, m: (i, 0, 0)),
                  pl.BlockSpec((1, tq, LANES), lambda i, m: (i, m, 0)),
                  pl.BlockSpec((1, nt, N_HEADS, tq), lambda i, m: (i, 0, 0, 0))],
        out_specs=pl.BlockSpec((1, tq, D_ATT), lambda i, m: (i, m, 0)),
        out_shape=jax.ShapeDtypeStruct((b, t, D_ATT), BF16),
        compiler_params=_cp("arbitrary", "arbitrary"), name="fox_prompt",
    )(q, k, v, cumc, cumr)


_KEY_NEG_INF = -(2 ** 31) + 0x7FFFFF


def _key_to_float(key):
    bits = jnp.where(key < 0, key ^ jnp.int32(0x7FFFFFFF), key)
    return jnp.where(key <= _KEY_NEG_INF, -jnp.inf, lax.bitcast_convert_type(bits, F32))


def _kth_largest(count_ge, rows, k):
    imin = jnp.int32(-2 ** 31)

    def body(i, t):
        cand = t + lax.shift_left(jnp.int32(1), jnp.int32(31) - i)
        return jnp.where(count_ge(_key_to_float(cand)) >= k, cand, t)

    t = lax.fori_loop(0, 32, body, jnp.full((rows, 1), imin, I32))
    return _key_to_float(t), _key_to_float(t + 1)


def _dsa_prompt_kernel(iq_ref, sm_ref, ki_ref, q_ref, k_ref, v_ref, o_ref, sc_scr, sel_scr, *, tq, topk):
    qi = pl.program_id(1)
    nkb = qi + 1
    qpos = qi * tq + _iota((tq, 1), 0)
    ncol = tq // LANES

    wsc = sm_ref[0] * (N_HEADS ** -0.5)
    iqs = []
    for j in range(D_ATT // LANES):
        iqs.extend(_split_pair(iq_ref[0, :, j * LANES:(j + 1) * LANES].astype(BF16)))

    def score_body(kb, _):
        ks = ki_ref[0, pl.ds(pl.multiple_of(kb * tq, tq), tq), :]
        acc = jnp.zeros((tq, tq), F32)
        for h in range(N_HEADS):
            d = jnp.maximum(_dot_nt(iqs[h], ks), 0.0)
            acc = acc + wsc[:, SM_IW + h:SM_IW + h + 1] * d
        kpos = kb * tq + _iota((1, tq), 1)
        sc_scr[kb] = jnp.where(kpos <= qpos, acc, -jnp.inf)
        return 0

    lax.fori_loop(0, nkb, score_body, 0)

    def count(fn):
        def body(kb, part):
            blk = sc_scr[kb]
            for c in range(ncol):
                part = part + jnp.where(fn(blk[:, c * LANES:(c + 1) * LANES]), 1.0, 0.0)
            return part

        part = lax.fori_loop(0, nkb, body, jnp.zeros((tq, LANES), F32))
        return jnp.sum(part, axis=1, keepdims=True)

    kf = float(topk)
    thr, nxt = _kth_largest(lambda cand: count(lambda x: x >= cand), tq, kf)
    need = kf - count(lambda x: x >= nxt)
    strict = _tri(tq, lambda r, c: r < c)

    def sel_body(kb, run):
        blk = sc_scr[kb]
        tie = (blk >= thr) & (blk < nxt)
        tief = jnp.where(tie, 1.0, 0.0)
        before = _dot(tief.astype(BF16), strict) + run
        kpos = kb * tq + _iota((1, tq), 1)
        sel = ((blk >= nxt) | (tie & (before < need))) & (kpos <= qpos)
        sel_scr[kb] = jnp.where(sel, 0.0, NEG)
        return run + jnp.sum(tief, axis=1, keepdims=True)

    lax.fori_loop(0, nkb, sel_body, jnp.zeros((tq, 1), F32))

    outs = []
    for j in range(D_ATT // LANES):
        sl = slice(j * LANES, (j + 1) * LANES)

        def k_of(kb, sl=sl):
            return k_ref[0, pl.ds(pl.multiple_of(kb * tq, tq), tq), sl]

        def v_of(kb, sl=sl):
            return v_ref[0, pl.ds(pl.multiple_of(kb * tq, tq), tq), sl]

        outs.append(_flash_pair(q_ref[0, :, sl], k_of, v_of, lambda kb, which: sel_scr[kb], nkb, tq))
    o_ref[0] = jnp.concatenate(outs, axis=1).astype(BF16)


def _dsa_prompt(u, kid, q, k, v, tq, topk):
    b, t, _ = q.shape
    nt = t // tq
    assert tq >= topk
    return pl.pallas_call(
        functools.partial(_dsa_prompt_kernel, tq=tq, topk=topk), grid=(b, nt),
        in_specs=[pl.BlockSpec((1, tq, BLK), lambda i, m: (i, m, BLK_IQ)),
                  pl.BlockSpec((1, tq, LANES), lambda i, m: (i, m, BLK_LAST * (BLK // LANES) + 1)),
                  pl.BlockSpec((1, t, LANES), lambda i, m: (i, 0, 0)),
                  pl.BlockSpec((1, tq, D_ATT), lambda i, m: (i, m, 0)),
                  pl.BlockSpec((1, t, D_ATT), lambda i, m: (i, 0, 0)),
                  pl.BlockSpec((1, t, D_ATT), lambda i, m: (i, 0, 0))],
        out_specs=pl.BlockSpec((1, tq, D_ATT), lambda i, m: (i, m, 0)),
        out_shape=jax.ShapeDtypeStruct((b, t, D_ATT), BF16),
        scratch_shapes=[pltpu.VMEM((nt, tq, tq), F32), pltpu.VMEM((nt, tq, tq), F32)],
        compiler_params=_cp("arbitrary", "arbitrary"), name="dsa_prompt",
    )(u, u, kid, q, k, v)


def _fox_bias_kernel(pt_ref, nlf_ref, *refs, pg):
    lf_refs = refs[:pg]
    col_o, r_o, run_scr = refs[pg:]
    s = pl.program_id(1)
    strict = _tri(LANES, lambda r, c: r > c)

    @pl.when(s == 0)
    def _():
        nlf = nlf_ref[0]
        r_o[0] = _dot3_r(nlf, strict)
        run_scr[...] = jnp.broadcast_to(jnp.sum(nlf, axis=1, keepdims=True), run_scr.shape)

    run = run_scr[...]
    for i in range(pg):
        lf = lf_refs[i][0, 0]
        col_o[0, pg - 1 - i] = -(_dot3_r(lf, strict) + run)
        run = run + jnp.sum(lf, axis=1, keepdims=True)
    run_scr[...] = run


def _fox_bias(pt, lfc_t, layer, nlf, pg):
    b, npg = pt.shape
    nst = npg // pg

    def page(i):
        return lambda bi, s, pt: (layer, pt[bi, npg - 1 - (s * pg + i)], 0, 0)

    return pl.pallas_call(
        functools.partial(_fox_bias_kernel, pg=pg),
        grid_spec=pltpu.PrefetchScalarGridSpec(
            num_scalar_prefetch=1, grid=(b, nst),
            in_specs=[pl.BlockSpec((1, N_HEADS, LANES), lambda bi, s, pt: (bi, 0, 0))]
            + [pl.BlockSpec((1, 1, N_HEADS, LANES), page(i)) for i in range(pg)],
            out_specs=[pl.BlockSpec((1, pg, N_HEADS, LANES), lambda bi, s, pt: (bi, nst - 1 - s, 0, 0)),
                       pl.BlockSpec((1, N_HEADS, LANES), lambda bi, s, pt: (bi, 0, 0))],
            scratch_shapes=[pltpu.VMEM((N_HEADS, LANES), F32)]),
        out_shape=[jax.ShapeDtypeStruct((b, npg, N_HEADS, LANES), F32),
                   jax.ShapeDtypeStruct((b, N_HEADS, LANES), F32)],
        compiler_params=_cp("arbitrary", "arbitrary"), name="fox_bias",
    )(pt, nlf, *([lfc_t] * pg))


def _index_scores(dots, w, t_new):
    d = jnp.maximum(dots, 0.0) * w
    return jnp.sum(d.reshape(t_new, N_HEADS, d.shape[1]), axis=1)


def _idx_score_kernel(pt_ref, iq_ref, w_ref, *refs, pg, t_new):
    kit_refs = refs[:pg]
    o_ref = refs[pg]
    iq = (iq_ref[0] * (HEAD_DIM ** -0.5)).astype(BF16)
    w = w_ref[0] * (N_HEADS ** -0.5)
    pad = jnp.full((8 - t_new, LANES), -jnp.inf, F32)
    for i in range(pg):
        sc = _index_scores(_dot(iq, kit_refs[i][0, 0].astype(BF16)), w, t_new)
        o_ref[0, i] = jnp.concatenate([sc, pad], axis=0)


def _idx_scores(pt, kic_t, layer, iq, w, t_new, pg):
    b, npg = pt.shape
    rows = t_new * N_HEADS

    def page(i):
        return lambda bi, s, pt: (layer, pt[bi, s * pg + i], 0, 0)

    return pl.pallas_call(
        functools.partial(_idx_score_kernel, pg=pg, t_new=t_new),
        grid_spec=pltpu.PrefetchScalarGridSpec(
            num_scalar_prefetch=1, grid=(b, npg // pg),
            in_specs=[pl.BlockSpec((1, rows, HEAD_DIM), lambda bi, s, pt: (bi, 0, 0)),
                      pl.BlockSpec((1, rows, 1), lambda bi, s, pt: (bi, 0, 0))]
            + [pl.BlockSpec((1, 1, HEAD_DIM, LANES), page(i)) for i in range(pg)],
            out_specs=pl.BlockSpec((1, pg, 8, LANES), lambda bi, s, pt: (bi, s, 0, 0))),
        out_shape=jax.ShapeDtypeStruct((b, npg, 8, LANES), F32),
        compiler_params=_cp("arbitrary", "arbitrary"), name="dsa_idx_scores",
    )(pt, iq, w, *([kic_t] * pg))


def _select_kernel(sc_ref, iq_ref, w_ref, kin_ref, mf_o, mn_o, sc_scr, cnt_scr, *, t_new, topk):
    npg = sc_ref.shape[1]
    n1 = npg + 1
    sc_scr[0:npg] = sc_ref[0]
    iq = (iq_ref[0] * (HEAD_DIM ** -0.5)).astype(BF16)
    w = w_ref[0] * (N_HEADS ** -0.5)
    scn = _index_scores(_dot_nt(iq, kin_ref[0].astype(BF16)), w, t_new)
    scn = jnp.where(_iota(scn.shape, 1) <= _iota(scn.shape, 0), scn, -jnp.inf)
    sc_scr[npg] = jnp.concatenate([scn, jnp.full((8 - t_new, LANES), -jnp.inf, F32)], axis=0)
    sc = sc_scr[...]

    def count(pred):
        part = jnp.sum(jnp.where(pred, 1.0, 0.0), axis=0)
        return jnp.sum(part, axis=1, keepdims=True)

    kf = float(topk)
    thr, nxt = _kth_largest(lambda cand: count(sc >= cand), 8, kf)
    need = kf - count(sc >= nxt)
    tie = (sc >= thr) & (sc < nxt)
    tief = jnp.where(tie, 1.0, 0.0)
    strict = _tri(LANES, lambda r, c: r < c)
    before = _dot(tief.reshape(n1 * 8, LANES).astype(BF16), strict).reshape(n1, 8, LANES)
    cnt_scr[...] = jnp.sum(tief, axis=2, keepdims=True)

    def body(p, run):
        c = cnt_scr[p]
        cnt_scr[p] = run
        return run + c

    lax.fori_loop(0, n1, body, jnp.zeros((8, 1), F32))
    sel = ((sc >= nxt) | (tie & (before + cnt_scr[...] < need))) & (sc > -jnp.inf)
    memb = jnp.where(sel, 1.0, 0.0)
    mf_o[0] = memb[0:npg]
    mn_o[0] = memb[npg]


def _dsa_select(scores, iq, w, ki_new, t_new, topk):
    b, npg, _, _ = scores.shape
    rows = t_new * N_HEADS
    return pl.pallas_call(
        functools.partial(_select_kernel, t_new=t_new, topk=topk), grid=(b,),
        in_specs=[pl.BlockSpec((1, npg, 8, LANES), lambda i: (i, 0, 0, 0)),
                  pl.BlockSpec((1, rows, HEAD_DIM), lambda i: (i, 0, 0)),
                  pl.BlockSpec((1, rows, 1), lambda i: (i, 0, 0)),
                  pl.BlockSpec((1, LANES, HEAD_DIM), lambda i: (i, 0, 0))],
        out_specs=[pl.BlockSpec((1, npg, 8, LANES), lambda i: (i, 0, 0, 0)),
                   pl.BlockSpec((1, 8, LANES), lambda i: (i, 0, 0))],
        out_shape=[jax.ShapeDtypeStruct((b, npg, 8, LANES), F32),
                   jax.ShapeDtypeStruct((b, 8, LANES), F32)],
        scratch_shapes=[pltpu.VMEM((npg + 1, 8, LANES), F32), pltpu.VMEM((npg + 1, 8, 1), F32)],
        compiler_params=_cp("arbitrary"), name="dsa_select",
    )(scores, iq, w, ki_new)


def _decode_kernel(pt_ref, q_ref, kn_ref, vn_ref, *refs, pg, nsteps, t_new, fox):
    kt_refs = refs[:pg]
    vt_refs = refs[pg:2 * pg]
    if fox:
        row_ref, col_ref, coln_ref, o_ref, qbd_scr, m_scr, l_scr, acc_scr = refs[2 * pg:]
    else:
        msk_ref, mskn_ref, o_ref, qbd_scr, m_scr, l_scr, acc_scr = refs[2 * pg:]
    s = pl.program_id(1)
    rows = t_new * N_HEADS
    lane = _iota((N_HEADS, D_ATT), 1)
    hrow = _iota((N_HEADS, D_ATT), 0)
    hmask = (lane >= hrow * HEAD_DIM) & (lane < (hrow + 1) * HEAD_DIM)

    def per_query(me):
        return jnp.concatenate([jnp.broadcast_to(me[t:t + 1, :], (N_HEADS, LANES)) for t in range(t_new)], axis=0)

    def per_head(me):
        return jnp.concatenate([me] * t_new, axis=0)

    @pl.when(s == 0)
    def _():
        q = q_ref[0].astype(F32) * (HEAD_DIM ** -0.5)
        for t in range(t_new):
            qt = jnp.broadcast_to(q[t:t + 1, :], (N_HEADS, D_ATT))
            qbd_scr[t * N_HEADS:(t + 1) * N_HEADS, :] = jnp.where(hmask, qt, 0.0).astype(BF16)
        sn = _dot_nt(qbd_scr[...], kn_ref[0])
        if fox:
            causal = _iota((rows, LANES), 1) <= lax.shift_right_logical(_iota((rows, LANES), 0), 3)
            sn = sn + jnp.where(causal, row_ref[0] - per_head(coln_ref[0]), NEG)
        else:
            sn = sn + jnp.where(per_query(mskn_ref[0]) > 0.5, 0.0, NEG)
        m = jnp.max(sn, axis=1, keepdims=True)
        p = jnp.exp(sn - m)
        m_scr[...] = m
        l_scr[...] = jnp.sum(p, axis=1, keepdims=True)
        acc_scr[...] = _dot(p.astype(BF16), vn_ref[0])

    qbd = qbd_scr[...]
    ss = []
    for i in range(pg):
        if fox:
            bias = row_ref[0] - per_head(col_ref[0, i])
        else:
            bias = jnp.where(per_query(msk_ref[0, i]) > 0.5, 0.0, NEG)
        ss.append(_dot(qbd, kt_refs[i][0, 0].astype(BF16)) + bias)
    m_old = m_scr[...]
    m_new = m_old
    for x in ss:
        m_new = jnp.maximum(m_new, jnp.max(x, axis=1, keepdims=True))
    alpha = jnp.exp(m_old - m_new)
    l = alpha * l_scr[...]
    acc = alpha * acc_scr[...]
    for i, x in enumerate(ss):
        p = jnp.exp(x - m_new)
        l = l + jnp.sum(p, axis=1, keepdims=True)
        acc = acc + _dot_nt(p.astype(BF16), vt_refs[i][0, 0].astype(BF16))
    m_scr[...] = m_new
    l_scr[...] = l
    acc_scr[...] = acc

    @pl.when(s == nsteps - 1)
    def _():
        out = acc / l
        ys = []
        for t in range(t_new):
            blk = jnp.where(hmask, out[t * N_HEADS:(t + 1) * N_HEADS, :], 0.0)
            ys.append(jnp.sum(blk, axis=0, keepdims=True))
        o_ref[0] = jnp.concatenate(ys, axis=0).astype(BF16)


def _decode_attn(pt, q, kc_t, vc_t, layer, k_new, v_new, bias_args, t_new, pg, fox):
    b, npg = pt.shape
    rows = t_new * N_HEADS
    nst = npg // pg

    def page(i):
        return lambda bi, s, pt: (layer, pt[bi, s * pg + i], 0, 0)

    def per_b(shape):
        return pl.BlockSpec((1,) + shape, lambda bi, s, pt: (bi,) + (0,) * len(shape))

    cache_specs = [pl.BlockSpec((1, 1, D_ATT, LANES), page(i)) for i in range(pg)]
    paged = pl.BlockSpec((1, pg, 8, LANES), lambda bi, s, pt: (bi, s, 0, 0))
    bias_specs = ([per_b((rows, 1)), paged, per_b((8, LANES))] if fox else [paged, per_b((8, LANES))])
    return pl.pallas_call(
        functools.partial(_decode_kernel, pg=pg, nsteps=nst, t_new=t_new, fox=fox),
        grid_spec=pltpu.PrefetchScalarGridSpec(
            num_scalar_prefetch=1, grid=(b, nst),
            in_specs=[per_b((t_new, D_ATT)), per_b((LANES, D_ATT)), per_b((LANES, D_ATT))]
            + cache_specs + cache_specs + bias_specs,
            out_specs=per_b((t_new, D_ATT)),
            scratch_shapes=[pltpu.VMEM((rows, D_ATT), BF16), pltpu.VMEM((rows, 1), F32),
                            pltpu.VMEM((rows, 1), F32), pltpu.VMEM((rows, D_ATT), F32)]),
        out_shape=jax.ShapeDtypeStruct((b, t_new, D_ATT), BF16),
        compiler_params=_cp("arbitrary", "arbitrary"), name="decode_attn",
    )(pt, q, k_new, v_new, *([kc_t] * pg), *([vc_t] * pg), *bias_args)


def _pack_w_in(w_in):
    o = 0
    seg = {}
    for name, n in (('z', 1024), ('xbc', 1536), ('dt', 16), ('fq', 512), ('fk', 512), ('fv', 512), ('ff', 8),
                    ('dq', 512), ('dk', 512), ('dv', 512), ('iq', 512), ('ik', 64), ('iw', 8)):
        seg[name] = w_in[:, :, o:o + n]
        o += n
    used = 1024 + 1536 + 7 * 512 + 64 + 64 + 16 + 8 + 8
    pad = jnp.zeros(w_in.shape[:2] + (N_PACK - used,), w_in.dtype)
    parts = [seg[k] for k in ('z', 'xbc', 'fq', 'fk', 'fv', 'dq', 'dk', 'dv', 'iq', 'ik', 'ik', 'dt', 'ff', 'iw')]
    return jnp.concatenate(parts + [pad], axis=-1).astype(BF16)


def _lane_row(vals, offset):
    return jnp.zeros((1, LANES), F32).at[0, offset:offset + vals.shape[0]].set(vals.astype(F32))


def _layer_consts(l, P):
    a = -jnp.exp(P['ssd_a_log'][l].astype(F32))
    dtb_row = _lane_row(P['ssd_dt_bias'][l], SM_DT)
    a_row = _lane_row(a, SM_DT)
    gmat = jnp.kron(jnp.eye(N_HEADS, dtype=F32), jnp.full((HEAD_DIM, HEAD_DIM), 1.0 / HEAD_DIM, F32)).astype(BF16)

    def tile_heads(g):
        return jnp.tile(g.astype(F32), N_HEADS).reshape(1, D_ATT)

    return dict(
        ssd_conv_w=P['ssd_conv_w'][l], ssd_conv_b=P['ssd_conv_b'][l].reshape(1, -1),
        dtb_row=dtb_row, dtb_col=dtb_row.reshape(LANES, 1), a_row=a_row, a_col=a_row.reshape(LANES, 1),
        d_exp=jnp.repeat(P['ssd_d'][l].astype(F32), HEAD_DIM).reshape(1, D_SSD),
        ssd_norm=P['ssd_norm'][l].reshape(1, D_SSD), gmat=gmat,
        fox_q_norm=tile_heads(P['fox_q_norm'][l]), fox_k_norm=tile_heads(P['fox_k_norm'][l]),
        dsa_q_norm=tile_heads(P['dsa_q_norm'][l]), dsa_k_norm=tile_heads(P['dsa_k_norm'][l]),
        fb_row=_lane_row(P['fox_f_bias'][l], SM_FF),
    )


def _pad_rows(x, rows):
    return jnp.pad(x, ((0, 0), (0, rows - x.shape[1]), (0, 0)))


def _layer(l, x, mod, P, wl, past):
    b, t, d = x.shape
    sh1, sc1, g1, sh2, sc2, g2 = [mod[:, None, i * d:(i + 1) * d] for i in range(6)]
    prompt = past is None
    m = b * t
    f2 = P['w_up'].shape[-1]

    h = _norm_mod(x, P['attn_norm'][l], sc1, sh1)
    if prompt:
        u = _matmul(h, P['w_in_p'][l], 512, BLK)
        uf = u
    else:
        uf = _matmul(h.reshape(1, m, d), P['w_in_p'][l], m, BLK)
        u = uf.reshape(b, t, N_PACK)

    if prompt:
        y_ssd, st = _ssd(u, wl, SSD_CHUNK)
    else:
        prev = jnp.pad(past['state_ssd_conv'][l], ((0, 0), (8 - (SSD_CONV - 1), 0), (0, 0)))
        h0 = past['state_ssd'][l].reshape(b, SSD_NHEADS * HEAD_DIM, SSD_STATE)
        y_ssd, st = _ssd(_pad_rows(u, SSD_CHUNK), wl, t, prev, h0)
        y_ssd = y_ssd[:, :t]
    ssd_new = st.reshape(b, SSD_NHEADS, HEAD_DIM, SSD_STATE)
    xbc_raw = u[:, :, BLK_Z * BLK + D_SSD:BLK_Z * BLK + D_SSD + 3 * BLK]
    if prompt:
        ssd_conv_new = xbc_raw[:, t - (SSD_CONV - 1):]
    else:
        ssd_conv_new = jnp.concatenate([past['state_ssd_conv'][l], xbc_raw], axis=1)[:, t:]

    tq = 256
    (fqn, fkn, fkb, fvb, dqn, dkn, dkb, dvb, kid, logf, *cum) = _prep(uf, wl, tq, prompt)
    fk_out = fkn.reshape(b, t, N_HEADS, HEAD_DIM)
    dk_out = dkn.reshape(b, t, N_HEADS, HEAD_DIM)
    fv_out = u[:, :, BLK_FV * BLK:(BLK_FV + 1) * BLK].reshape(b, t, N_HEADS, HEAD_DIM)
    dv_out = u[:, :, BLK_DV * BLK:(BLK_DV + 1) * BLK].reshape(b, t, N_HEADS, HEAD_DIM)
    ik_out = u[:, :, BLK_LAST * BLK:BLK_LAST * BLK + HEAD_DIM]
    logf_out = logf.reshape(b, t, LANES)[:, :, SM_FF:SM_FF + N_HEADS]

    if prompt:
        cumc, cumr = cum
        y_fox = _fox_prompt(fqn, fkb, fvb, cumc, cumr, tq)
        y_dsa = _dsa_prompt(u, kid, dqn, dkb, dvb, tq, min(TOPK_MAX, t // 4))
    else:
        pt = past['page_table']
        npg = pt.shape[1]
        page = past['kidx_t'].shape[3]
        assert page == LANES
        rows = t * N_HEADS
        pg_kv = min(PAGES_PER_STEP_KV, npg)
        pg_aux = min(PAGES_PER_STEP_AUX, npg)

        def r3(a):
            return a.reshape(b, t, -1)

        def new_page(a):
            return _pad_rows(r3(a), LANES)

        nlf = jnp.pad(jnp.swapaxes(logf_out, 1, 2), ((0, 0), (0, 0), (0, LANES - t)))
        col, rnew = _fox_bias(pt, past['fox_logf_t'], l, nlf, pg_aux)
        row_term = jnp.swapaxes(-rnew[:, :, :t], 1, 2).reshape(b, rows, 1)
        y_fox = _decode_attn(pt, r3(fqn), past['fox_kt'], past['fox_vt'], l, new_page(fkb), new_page(fvb),
                             (row_term, col, -rnew), t, pg_kv, True)

        iq = u[:, :, BLK_IQ * BLK:(BLK_IQ + 1) * BLK].reshape(b, rows, HEAD_DIM)
        iw = u[:, :, BLK_LAST * BLK + LANES + SM_IW:BLK_LAST * BLK + LANES + SM_IW + N_HEADS].reshape(b, rows, 1)
        sc = _idx_scores(pt, past['kidx_t'], l, iq, iw, t, pg_aux)
        memb, memb_new = _dsa_select(sc, iq, iw, _pad_rows(ik_out, LANES), t,
                                     min(TOPK_MAX, (npg * page + t) // 4))
        y_dsa = _decode_attn(pt, r3(dqn), past['dsa_kt'], past['dsa_vt'], l, new_page(dkb), new_page(dvb),
                             (memb, memb_new), t, pg_kv, False)

    mixed = jnp.concatenate([y_ssd, y_fox.reshape(b, t, D_ATT), y_dsa.reshape(b, t, D_ATT)], axis=-1)
    if prompt:
        x = _matmul(mixed, P['w_out_b'][l], 512, 512, res=x, gate=g1)
    else:
        x = _matmul(mixed.reshape(1, m, d), P['w_out_b'][l], m, 512, res=x.reshape(1, m, d),
                    gate=jnp.broadcast_to(g1, (b, t, d)).reshape(1, m, d)).reshape(b, t, d)

    h2 = _norm_mod(x, P['ffn_norm'][l], sc2, sh2)
    cw = P['ffn_conv_w'][l]
    cb = P['ffn_conv_b'][l].reshape(1, f2)
    if prompt:
        act, ta, tg = _ffn_up_prompt(h2.reshape(m, d), P['w_up_b'][l], cw, cb, t)
        ffn_conv_new = jnp.concatenate([ta, tg], axis=-1)[:, 8 - (FFN_CONV - 1):]
        x = _matmul(act.reshape(b, t, -1), P['w_down_b'][l], 512, 512, res=x, gate=g2)
    else:
        buf = past['state_ffn_conv'][l]
        zero = jnp.zeros((b, t - 1, f2), F32)
        p1 = jnp.concatenate([buf[:, 1:2], zero], axis=1).reshape(m, f2)
        p2 = jnp.concatenate([buf[:, 0:2], zero[:, 1:]], axis=1).reshape(m, f2)
        act, ra, rg = _ffn_up_sample(h2.reshape(m, d), P['w_up_b'][l], cw, cb, p1, p2, t)
        raw = jnp.concatenate([ra, rg], axis=-1).reshape(b, t, f2)
        ffn_conv_new = jnp.concatenate([buf, raw], axis=1)[:, t:]
        x = _matmul(act.reshape(1, m, -1), P['w_down_b'][l], m, 512, res=x.reshape(1, m, d),
                    gate=jnp.broadcast_to(g2, (b, t, d)).reshape(1, m, d)).reshape(b, t, d)

    new = (fk_out, fv_out, logf_out, dk_out, dv_out, ik_out, ssd_new, ssd_conv_new, ffn_conv_new)
    return x, new


def kernel(x_prompt, x_sample, cache_fox_k, cache_fox_v, cache_fox_logf, cache_dsa_k, cache_dsa_v, cache_dsa_kidx, state_ssd, state_ssd_conv, state_ffn_conv, page_table, c_prompt, c_sample, w_ada, b_ada, attn_norm, w_in, ssd_conv_w, ssd_conv_b, ssd_dt_bias, ssd_a_log, ssd_d, ssd_norm, fox_q_norm, fox_k_norm, fox_f_bias, dsa_q_norm, dsa_k_norm, w_out, ffn_norm, w_up, ffn_conv_w, ffn_conv_b, w_down):
    depth = w_in.shape[0]
    bp = x_prompt.shape[0]
    bs = x_sample.shape[0]
    P = dict(attn_norm=attn_norm, ssd_conv_w=ssd_conv_w, ssd_conv_b=ssd_conv_b, ssd_dt_bias=ssd_dt_bias,
             ssd_a_log=ssd_a_log, ssd_d=ssd_d, ssd_norm=ssd_norm, fox_q_norm=fox_q_norm, fox_k_norm=fox_k_norm,
             fox_f_bias=fox_f_bias, dsa_q_norm=dsa_q_norm, dsa_k_norm=dsa_k_norm, ffn_norm=ffn_norm,
             ffn_conv_w=ffn_conv_w, ffn_conv_b=ffn_conv_b, w_up=w_up,
             w_in_p=_pack_w_in(w_in), w_out_b=w_out.astype(BF16), w_up_b=w_up.astype(BF16),
             w_down_b=w_down.astype(BF16))
    npool, page = cache_fox_k.shape[1], cache_fox_k.shape[2]

    def positions_minor(c):
        return jnp.transpose(c, (0, 1, 3, 4, 2)).reshape(depth, npool, N_HEADS * HEAD_DIM, page)

    past = dict(state_ssd=state_ssd, state_ssd_conv=state_ssd_conv, state_ffn_conv=state_ffn_conv,
                page_table=page_table,
                fox_kt=positions_minor(cache_fox_k), fox_vt=positions_minor(cache_fox_v),
                dsa_kt=positions_minor(cache_dsa_k), dsa_vt=positions_minor(cache_dsa_v),
                kidx_t=jnp.swapaxes(cache_dsa_kidx, 2, 3),
                fox_logf_t=jnp.swapaxes(cache_fox_logf, 2, 3))

    rows = bp + bs
    rows_pad = -(-rows // 8) * 8
    c_all = jnp.pad(jnp.concatenate([c_prompt, c_sample], axis=0), ((0, rows_pad - rows), (0, 0)))
    mods = _ada_mod(c_all, w_ada, b_ada)

    xp, xs = x_prompt, x_sample
    new_p, new_s = [], []
    for l in range(depth):
        wl = _layer_consts(l, P)
        xp, sp = _layer(l, xp, mods[l, :bp], P, wl, None)
        xs, ss = _layer(l, xs, mods[l, bp:rows], P, wl, past)
        new_p.append(sp)
        new_s.append(ss)
    outs_p = [jnp.stack(a, 0) for a in zip(*new_p)]
    outs_s = [jnp.stack(a, 0) for a in zip(*new_s)]
    return tuple([xp, xs] + outs_p + outs_s)
```

```python
import functools

import jax
import jax.numpy as jnp
from jax import lax
from jax.experimental import pallas as pl
from jax.experimental.pallas import tpu as pltpu

F32 = jnp.float32
BF16 = jnp.bfloat16
I32 = jnp.int32

RMS_EPS = 1e-6
NEG = -1e30
HEAD_DIM = 64
LANES = 128
D_SSD = 1024
SSD_STATE = 128
SSD_NHEADS = 16
SSD_CONV = 4
SSD_CHUNK = 128
D_ATT = 512
N_HEADS = 8
FFN_CONV = 3
TOPK_MAX = 256
VMEM_LIMIT = 56 * 1024 * 1024
PAGES_PER_STEP_KV = 16
PAGES_PER_STEP_AUX = 16

BLK = 512
BLK_Z, BLK_X, BLK_FQ, BLK_FK, BLK_FV, BLK_DQ, BLK_DK, BLK_DV, BLK_IQ, BLK_LAST = 0, 2, 5, 6, 7, 8, 9, 10, 11, 12
N_PACK = 13 * BLK
TN_IN = N_PACK // 4
TN_DOWN = 1024
SM_DT, SM_FF, SM_IW = 0, 16, 24


def _cp(*sem):
    return pltpu.CompilerParams(dimension_semantics=sem, vmem_limit_bytes=VMEM_LIMIT)


def _dot(a, b):
    return jnp.dot(a, b, preferred_element_type=F32)


def _dot_nt(a, b):
    return lax.dot_general(a, b, (((1,), (1,)), ((), ())), preferred_element_type=F32)


def _split3(x):
    hi = x.astype(BF16)
    r = x - hi.astype(F32)
    mid = r.astype(BF16)
    lo = (r - mid.astype(F32)).astype(BF16)
    return hi, mid, lo


def _dot3_l(m01, x):
    hi, mid, lo = _split3(x)
    return _dot(m01, hi) + _dot(m01, mid) + _dot(m01, lo)


def _dot3_r(x, m01):
    hi, mid, lo = _split3(x)
    return _dot(hi, m01) + _dot(mid, m01) + _dot(lo, m01)


def _softplus(x):
    return jnp.maximum(x, 0.0) + jnp.log1p(jnp.exp(-jnp.abs(x)))


def _silu(x):
    return x * jax.nn.sigmoid(x)


def _iota(shape, dim):
    return lax.broadcasted_iota(I32, shape, dim)


def _tri(n, fn):
    r = _iota((n, n), 0)
    c = _iota((n, n), 1)
    return jnp.where(fn(r, c), 1.0, 0.0).astype(BF16)


def _ada_kernel(c_ref, w_ref, b_ref, o_ref):
    s = _silu(c_ref[...]).astype(BF16)
    o_ref[0] = _dot(s, w_ref[0].astype(BF16)) + b_ref[0]


def _ada_mod(c_all, w_ada, b_ada):
    nl, d, n = w_ada.shape
    rows = c_all.shape[0]
    tn = 1024
    return pl.pallas_call(
        _ada_kernel, grid=(nl, n // tn),
        in_specs=[pl.BlockSpec((rows, d), lambda l, j: (0, 0)),
                  pl.BlockSpec((1, d, tn), lambda l, j: (l, 0, j)),
                  pl.BlockSpec((1, 1, tn), lambda l, j: (l, 0, j))],
        out_specs=pl.BlockSpec((1, rows, tn), lambda l, j: (l, 0, j)),
        out_shape=jax.ShapeDtypeStruct((nl, rows, n), F32),
        compiler_params=_cp("arbitrary", "arbitrary"), name="ada_mod",
    )(c_all, w_ada, b_ada.reshape(nl, 1, n))


def _normmod_kernel(x_ref, g_ref, sc_ref, sh_ref, o_ref):
    x = x_ref[0]
    ms = jnp.mean(x * x, axis=-1, keepdims=True)
    y = x * lax.rsqrt(ms + RMS_EPS) * g_ref[...]
    o_ref[0] = (y * (1.0 + sc_ref[0]) + sh_ref[0]).astype(BF16)


def _norm_mod(x, g, sc, sh):
    b, t, d = x.shape
    tm = min(t, 512)
    return pl.pallas_call(
        _normmod_kernel, grid=(b, t // tm),
        in_specs=[pl.BlockSpec((1, tm, d), lambda i, m: (i, m, 0)),
                  pl.BlockSpec((1, d), lambda i, m: (0, 0)),
                  pl.BlockSpec((1, 1, d), lambda i, m: (i, 0, 0)),
                  pl.BlockSpec((1, 1, d), lambda i, m: (i, 0, 0))],
        out_specs=pl.BlockSpec((1, tm, d), lambda i, m: (i, m, 0)),
        out_shape=jax.ShapeDtypeStruct((b, t, d), BF16),
        compiler_params=_cp("arbitrary", "arbitrary"), name="norm_mod",
    )(x, g.reshape(1, d), sc, sh)


def _mm_kernel(x_ref, w_ref, o_ref):
    o_ref[0] = _dot(x_ref[0], w_ref[...])


def _mm_res_kernel(x_ref, w_ref, r_ref, g_ref, o_ref):
    o_ref[0] = r_ref[0] + g_ref[0] * _dot(x_ref[0], w_ref[...])


def _mm3_res_kernel(x1_ref, x2_ref, x3_ref, w1_ref, w2_ref, w3_ref, r_ref, g_ref, o_ref):
    acc = _dot(x1_ref[0], w1_ref[...]) + _dot(x2_ref[0], w2_ref[...]) + _dot(x3_ref[0], w3_ref[...])
    o_ref[0] = r_ref[0] + g_ref[0] * acc


def _out_proj(xs, w, tm, res, gate):
    g, t, _ = xs[0].shape
    n = w.shape[1]
    tm = min(tm, t)
    widths = [x.shape[2] for x in xs]
    unit = widths[1]
    assert widths[2] == unit and widths[0] % unit == 0
    starts = [0, widths[0] // unit, widths[0] // unit + 1]

    def x_spec(k):
        return pl.BlockSpec((1, tm, k), lambda i, m: (i, m, 0))

    def w_spec(k, s):
        return pl.BlockSpec((k, n), lambda i, m: (s if k == unit else 0, 0))

    o_spec = pl.BlockSpec((1, tm, n), lambda i, m: (i, m, 0))
    if gate.shape[1] == 1:
        g_spec = pl.BlockSpec((1, 1, n), lambda i, m: (i, 0, 0))
    else:
        g_spec = o_spec
    return pl.pallas_call(
        _mm3_res_kernel, grid=(g, t // tm),
        in_specs=[x_spec(k) for k in widths] + [w_spec(k, s) for k, s in zip(widths, starts)] + [o_spec, g_spec],
        out_specs=o_spec, out_shape=jax.ShapeDtypeStruct((g, t, n), F32),
        compiler_params=_cp("arbitrary", "arbitrary"), name="out_proj",
    )(*xs, w, w, w, res, gate)


def _matmul(x, w, tm, tn, res=None, gate=None):
    g, t, k = x.shape
    n = w.shape[1]
    tm = min(tm, t)
    grid = (n // tn, g, t // tm)
    x_spec = pl.BlockSpec((1, tm, k), lambda j, i, m: (i, m, 0))
    w_spec = pl.BlockSpec((k, tn), lambda j, i, m: (0, j))
    o_spec = pl.BlockSpec((1, tm, tn), lambda j, i, m: (i, m, j))
    out_shape = jax.ShapeDtypeStruct((g, t, n), F32)
    cp = _cp("arbitrary", "arbitrary", "arbitrary")
    if res is None:
        return pl.pallas_call(_mm_kernel, grid=grid, in_specs=[x_spec, w_spec], out_specs=o_spec,
                              out_shape=out_shape, compiler_params=cp, name="proj")(x, w)
    if gate.shape[1] == 1:
        g_spec = pl.BlockSpec((1, 1, tn), lambda j, i, m: (i, 0, j))
    else:
        g_spec = pl.BlockSpec((1, tm, tn), lambda j, i, m: (i, m, j))
    return pl.pallas_call(_mm_res_kernel, grid=grid, in_specs=[x_spec, w_spec, o_spec, g_spec],
                          out_specs=o_spec, out_shape=out_shape, compiler_params=cp,
                          name="proj_res")(x, w, res, gate)


def _conv3(u, prev1, prev2, cw_ref, cb_ref):
    return cb_ref[...] + prev2 * cw_ref[0:1, :] + prev1 * cw_ref[1:2, :] + u * cw_ref[2:3, :]


def _shift_with_carry(u, carry):
    r1 = pltpu.roll(u, 1, 0)
    r2 = pltpu.roll(u, 2, 0)
    c1 = pltpu.roll(carry, 1, 0)
    c2 = pltpu.roll(carry, 2, 0)
    row = _iota((8, u.shape[1]), 0)
    p1 = jnp.concatenate([jnp.where(row < 1, c1, r1[0:8]), r1[8:]], axis=0)
    p2 = jnp.concatenate([jnp.where(row < 2, c2, r2[0:8]), r2[8:]], axis=0)
    return p1, p2


def _ffn_up_prompt_kernel(x_ref, wa_ref, wg_ref, cwa_ref, cwg_ref, cba_ref, cbg_ref,
                          act_ref, ta_ref, tg_ref, ca_scr, cg_scr, *, tiles_per_seq):
    m = pl.program_id(1)
    tm = x_ref.shape[0]

    @pl.when(m % tiles_per_seq == 0)
    def _():
        ca_scr[...] = jnp.zeros_like(ca_scr)
        cg_scr[...] = jnp.zeros_like(cg_scr)

    x = x_ref[...]
    ua = _dot(x, wa_ref[...])
    ug = _dot(x, wg_ref[...])
    pa1, pa2 = _shift_with_carry(ua, ca_scr[...])
    pg1, pg2 = _shift_with_carry(ug, cg_scr[...])
    a = _conv3(ua, pa1, pa2, cwa_ref, cba_ref)
    gt = _conv3(ug, pg1, pg2, cwg_ref, cbg_ref)
    act_ref[...] = (_silu(gt) * a).astype(BF16)
    ca_scr[...] = ua[tm - 8:tm]
    cg_scr[...] = ug[tm - 8:tm]
    ta_ref[0] = ua[tm - 8:tm]
    tg_ref[0] = ug[tm - 8:tm]


def _ffn_up_prompt(h, w_up, cw, cb, t_seq):
    m, k = h.shape
    f = w_up.shape[1] // 2
    tm, tn = 512, 512
    nj = f // tn
    nb = m // t_seq
    tps = t_seq // tm
    cw_spec_a = pl.BlockSpec((FFN_CONV, tn), lambda j, i: (0, j))
    cw_spec_g = pl.BlockSpec((FFN_CONV, tn), lambda j, i: (0, j + nj))
    cb_spec_a = pl.BlockSpec((1, tn), lambda j, i: (0, j))
    cb_spec_g = pl.BlockSpec((1, tn), lambda j, i: (0, j + nj))
    tail_spec = pl.BlockSpec((1, 8, tn), lambda j, i: (i // tps, 0, j))
    return pl.pallas_call(
        functools.partial(_ffn_up_prompt_kernel, tiles_per_seq=tps),
        grid=(nj, m // tm),
        in_specs=[pl.BlockSpec((tm, k), lambda j, i: (i, 0)),
                  pl.BlockSpec((k, tn), lambda j, i: (0, j)),
                  pl.BlockSpec((k, tn), lambda j, i: (0, j + nj)),
                  cw_spec_a, cw_spec_g, cb_spec_a, cb_spec_g],
        out_specs=[pl.BlockSpec((tm, tn), lambda j, i: (i, j)), tail_spec, tail_spec],
        out_shape=[jax.ShapeDtypeStruct((m, f), BF16),
                   jax.ShapeDtypeStruct((nb, 8, f), F32),
                   jax.ShapeDtypeStruct((nb, 8, f), F32)],
        scratch_shapes=[pltpu.VMEM((8, tn), F32), pltpu.VMEM((8, tn), F32)],
        compiler_params=_cp("arbitrary", "arbitrary"), name="ffn_up_prompt",
    )(h, w_up, w_up, cw, cw, cb, cb)


def _ffn_up_sample_kernel(x_ref, wa_ref, wg_ref, cwa_ref, cwg_ref, cba_ref, cbg_ref,
                          p1a_ref, p2a_ref, p1g_ref, p2g_ref, act_ref, ra_ref, rg_ref, *, t_seq):
    x = x_ref[...]
    ua = _dot(x, wa_ref[...])
    ug = _dot(x, wg_ref[...])
    t = _iota(ua.shape, 0) % t_seq

    def prevs(u, p1_ref, p2_ref):
        p1 = jnp.where(t >= 1, pltpu.roll(u, 1, 0), 0.0) + p1_ref[...]
        p2 = jnp.where(t >= 2, pltpu.roll(u, 2, 0), 0.0) + p2_ref[...]
        return p1, p2

    pa1, pa2 = prevs(ua, p1a_ref, p2a_ref)
    pg1, pg2 = prevs(ug, p1g_ref, p2g_ref)
    a = _conv3(ua, pa1, pa2, cwa_ref, cba_ref)
    gt = _conv3(ug, pg1, pg2, cwg_ref, cbg_ref)
    act_ref[...] = (_silu(gt) * a).astype(BF16)
    ra_ref[...] = ua
    rg_ref[...] = ug


def _ffn_up_sample(h, w_up, cw, cb, p1, p2, t_seq):
    m, k = h.shape
    f = w_up.shape[1] // 2
    tn = 512
    nj = f // tn
    a_spec = pl.BlockSpec((m, tn), lambda j: (0, j))
    g_spec = pl.BlockSpec((m, tn), lambda j: (0, j + nj))
    return pl.pallas_call(
        functools.partial(_ffn_up_sample_kernel, t_seq=t_seq),
        grid=(nj,),
        in_specs=[pl.BlockSpec((m, k), lambda j: (0, 0)),
                  pl.BlockSpec((k, tn), lambda j: (0, j)),
                  pl.BlockSpec((k, tn), lambda j: (0, j + nj)),
                  pl.BlockSpec((FFN_CONV, tn), lambda j: (0, j)),
                  pl.BlockSpec((FFN_CONV, tn), lambda j: (0, j + nj)),
                  pl.BlockSpec((1, tn), lambda j: (0, j)),
                  pl.BlockSpec((1, tn), lambda j: (0, j + nj)),
                  a_spec, a_spec, g_spec, g_spec],
        out_specs=[a_spec, a_spec, a_spec],
        out_shape=[jax.ShapeDtypeStruct((m, f), BF16),
                   jax.ShapeDtypeStruct((m, f), F32),
                   jax.ShapeDtypeStruct((m, f), F32)],
        compiler_params=_cp("arbitrary"), name="ffn_up_sample",
    )(h, w_up, w_up, cw, cw, cb, cb, p1, p2, p1, p2)


def _ssd_kernel(*refs, valid, has_init, nchunks):
    if has_init:
        (z0, z1, x0, x1, x2, sm_ref, cw_ref, cb_ref, dtb_row, dtb_col, a_row, a_col, d_ref, ng_ref,
         prev_ref, h0_ref, y_ref, st_ref, xx_scr, st_scr) = refs
    else:
        (z0, z1, x0, x1, x2, sm_ref, cw_ref, cb_ref, dtb_row, dtb_col, a_row, a_col, d_ref, ng_ref,
         y_ref, st_ref, xx_scr, st_scr) = refs
    c = pl.program_id(1)
    L = SSD_CHUNK
    npair = SSD_NHEADS // 2

    @pl.when(c == 0)
    def _():
        if has_init:
            xx_scr[0:8, :] = prev_ref[0]
            for j in range(npair):
                st_scr[j] = h0_ref[0, j * L:(j + 1) * L, :].T
        else:
            xx_scr[0:8, :] = jnp.zeros((8, xx_scr.shape[1]), F32)
            st_scr[...] = jnp.zeros_like(st_scr)

    xx_scr[8:8 + L, 0:BLK] = x0[0]
    xx_scr[8:8 + L, BLK:2 * BLK] = x1[0]
    xx_scr[8:8 + L, 2 * BLK:3 * BLK] = x2[0]
    acc = cb_ref[...] + xx_scr[5:5 + L, :] * cw_ref[0:1, :]
    for i in range(1, SSD_CONV):
        acc = acc + xx_scr[5 + i:5 + i + L, :] * cw_ref[i:i + 1, :]
    xx_scr[0:8, :] = xx_scr[L:L + 8, :]
    xbc = _silu(acc)
    xs = xbc[:, 0:D_SSD]
    bm = xbc[:, D_SSD:D_SSD + 2 * SSD_STATE]
    cm = xbc[:, D_SSD + 2 * SSD_STATE:D_SSD + 4 * SSD_STATE]

    small = sm_ref[0]
    dt_c = _softplus(small + dtb_row[...])
    small_t = small.T
    dt_r = _softplus(small_t + dtb_col[...])
    if valid < L:
        dt_c = jnp.where(_iota((L, LANES), 0) < valid, dt_c, 0.0)
        dt_r = jnp.where(_iota((LANES, L), 1) < valid, dt_r, 0.0)
    a_c = dt_c * a_row[...]
    a_r = dt_r * a_col[...]
    tril = _tri(L, lambda r, cc: r >= cc)
    triu = _tri(L, lambda r, cc: r <= cc)
    ones = jnp.ones((L, L), BF16)
    cs_c = _dot3_l(tril, a_c)
    cs_r = _dot3_r(a_r, triu)
    tot_c = _dot3_l(ones, a_c)
    e_cs = jnp.exp(cs_c)
    e_dec = jnp.exp(tot_c - cs_c)
    e_tot = jnp.exp(tot_c)

    lane = _iota((L, LANES), 1)
    first = lane < HEAD_DIM
    causal = _iota((L, L), 0) >= _iota((L, L), 1)

    def bcast(mat, h):
        return jnp.broadcast_to(mat[:, h:h + 1], (L, LANES))

    def pair(mat, h):
        return jnp.where(first, bcast(mat, h), bcast(mat, h + 1))

    ys = []
    for g in range(2):
        b_g = bm[:, g * SSD_STATE:(g + 1) * SSD_STATE]
        c_g = cm[:, g * SSD_STATE:(g + 1) * SSD_STATE].astype(BF16)
        cb_g = _dot_nt(c_g, b_g.astype(BF16))
        bt_g = b_g.T.astype(BF16)
        for jj in range(npair // 2):
            j = g * (npair // 2) + jj
            ha = 2 * j
            xs_p = xs[:, j * LANES:(j + 1) * LANES]
            x_p = xs_p * pair(dt_c, ha)
            x_pb = x_p.astype(BF16)
            yd = []
            for h in (ha, ha + 1):
                diff = jnp.minimum(bcast(cs_c, h) - cs_r[h:h + 1, :], 0.0)
                lm = jnp.where(causal, jnp.exp(diff), 0.0)
                yd.append(_dot((cb_g * lm).astype(BF16), x_pb))
            y_diag = jnp.where(first, yd[0], yd[1])
            st_old = st_scr[j]
            y_off = _dot(c_g, st_old.astype(BF16)) * pair(e_cs, ha)
            ys.append(y_diag + y_off + d_ref[:, j * LANES:(j + 1) * LANES] * xs_p)
            xd = (x_p * pair(e_dec, ha)).astype(BF16)
            st_scr[j] = pair(e_tot, ha) * st_old + _dot(bt_g, xd)
    y = jnp.concatenate(ys, axis=1)

    zg = jnp.concatenate([z0[0], z1[0]], axis=1)
    y = y * _silu(zg)
    half = D_SSD // 2
    outs = []
    for g in range(2):
        yg = y[:, g * half:(g + 1) * half]
        ms = jnp.mean(yg * yg, axis=-1, keepdims=True)
        outs.append(yg * lax.rsqrt(ms + RMS_EPS) * ng_ref[:, g * half:(g + 1) * half])
    y_ref[0] = jnp.concatenate(outs, axis=1).astype(BF16)

    @pl.when(c == nchunks - 1)
    def _():
        for j in range(npair):
            st_ref[0, j * L:(j + 1) * L, :] = st_scr[j].T


def _ssd(u, wl, valid, prev=None, h0=None):
    b, t, _ = u.shape
    L = SSD_CHUNK
    nch = t // L
    has_init = prev is not None

    def ublk(k):
        return pl.BlockSpec((1, L, BLK), lambda i, c: (i, c, k))

    def full(a):
        return pl.BlockSpec(a.shape, lambda i, c: (0,) * a.ndim)

    consts = [wl['ssd_conv_w'], wl['ssd_conv_b'], wl['dtb_row'], wl['dtb_col'], wl['a_row'], wl['a_col'],
              wl['d_exp'], wl['ssd_norm']]
    in_specs = [ublk(BLK_Z), ublk(BLK_Z + 1), ublk(BLK_X), ublk(BLK_X + 1), ublk(BLK_X + 2),
                pl.BlockSpec((1, L, LANES), lambda i, c: (i, c, BLK_LAST * (BLK // LANES) + 1))]
    in_specs += [full(a) for a in consts]
    args = [u, u, u, u, u, u] + consts
    if has_init:
        in_specs += [pl.BlockSpec((1, 8, 3 * BLK), lambda i, c: (i, 0, 0)),
                     pl.BlockSpec((1, SSD_NHEADS * HEAD_DIM, SSD_STATE), lambda i, c: (i, 0, 0))]
        args += [prev, h0]
    return pl.pallas_call(
        functools.partial(_ssd_kernel, valid=valid, has_init=has_init, nchunks=nch),
        grid=(b, nch), in_specs=in_specs,
        out_specs=[pl.BlockSpec((1, L, D_SSD), lambda i, c: (i, c, 0)),
                   pl.BlockSpec((1, SSD_NHEADS * HEAD_DIM, SSD_STATE), lambda i, c: (i, 0, 0))],
        out_shape=[jax.ShapeDtypeStruct((b, t, D_SSD), BF16),
                   jax.ShapeDtypeStruct((b, SSD_NHEADS * HEAD_DIM, SSD_STATE), F32)],
        scratch_shapes=[pltpu.VMEM((L + 8, 3 * BLK), F32),
                        pltpu.VMEM((SSD_NHEADS // 2, SSD_STATE, LANES), F32)],
        compiler_params=_cp("arbitrary", "arbitrary"), name="ssd",
    )(*args)


def _prep_kernel(*refs, do_cum):
    (fq, fk, fv, dq, dk, dv, last, gm, gfq, gfk, gdq, gdk, fb) = refs[:13]
    if do_cum:
        (fqn, fkn, fkb, fvb, dqn, dkn, dkb, dvb, kid, logf_o, cumc, cumr, fvt, dvt, carry) = refs[13:]
    else:
        (fqn, fkn, fkb, fvb, dqn, dkn, dkb, dvb, kid, logf_o) = refs[13:]
    m = pl.program_id(1)
    gmat = gm[...]

    def headnorm(x, g_ref):
        sq = x * x
        hi = sq.astype(BF16)
        lo = (sq - hi.astype(F32)).astype(BF16)
        ms = _dot(hi, gmat) + _dot(lo, gmat)
        return x * lax.rsqrt(ms + RMS_EPS) * g_ref[...]

    fqn[0] = headnorm(fq[0], gfq).astype(BF16)
    k1 = headnorm(fk[0], gfk)
    fkn[0] = k1
    fkb[0] = k1.astype(BF16)
    fvb[0] = fv[0].astype(BF16)
    dqn[0] = headnorm(dq[0], gdq).astype(BF16)
    k2 = headnorm(dk[0], gdk)
    dkn[0] = k2
    dkb[0] = k2.astype(BF16)
    dvb[0] = dv[0].astype(BF16)
    lst = last[0]
    kid[0] = lst[:, 0:LANES].astype(BF16)
    small = lst[:, LANES:2 * LANES]
    logf = -_softplus(-(small + fb[...]))
    logf_o[0] = logf
    if do_cum:
        tm = small.shape[0]

        @pl.when(m == 0)
        def _():
            carry[...] = jnp.zeros_like(carry)

        lane = _iota(small.shape, 1)
        lf = jnp.where((lane >= SM_FF) & (lane < SM_FF + N_HEADS), logf, 0.0)
        cum = _dot3_l(_tri(tm, lambda r, c: r >= c), lf) + carry[0:1, :]
        carry[...] = jnp.broadcast_to(cum[tm - 1:tm, :], carry.shape)
        cumc[0] = cum
        cumr[0, 0] = cum.T[SM_FF:SM_FF + N_HEADS, :]
        for v_in, vt_out in ((fv, fvt), (dv, dvt)):
            vv = v_in[0]
            for j in range(D_ATT // LANES):
                vt_out[0, j, 0] = vv[:, j * LANES:(j + 1) * LANES].T.astype(BF16)


def _prep(u, wl, tm, do_cum):
    g, t, _ = u.shape
    tm = min(tm, t)
    nt = t // tm

    def ublk(k):
        return pl.BlockSpec((1, tm, BLK), lambda i, m: (i, m, k))

    def full(a):
        return pl.BlockSpec(a.shape, lambda i, m: (0,) * a.ndim)

    consts = [wl['gmat'], wl['fox_q_norm'], wl['fox_k_norm'], wl['dsa_q_norm'], wl['dsa_k_norm'], wl['fb_row']]
    in_specs = [ublk(BLK_FQ), ublk(BLK_FK), ublk(BLK_FV), ublk(BLK_DQ), ublk(BLK_DK), ublk(BLK_DV),
                ublk(BLK_LAST)] + [full(a) for a in consts]
    o512 = pl.BlockSpec((1, tm, BLK), lambda i, m: (i, m, 0))
    o128 = pl.BlockSpec((1, tm, LANES), lambda i, m: (i, m, 0))

    def s512(dt):
        return jax.ShapeDtypeStruct((g, t, BLK), dt)

    out_specs = [o512] * 8 + [o128, o128]
    out_shape = [s512(BF16), s512(F32), s512(BF16), s512(BF16), s512(BF16), s512(F32), s512(BF16), s512(BF16),
                 jax.ShapeDtypeStruct((g, t, LANES), BF16), jax.ShapeDtypeStruct((g, t, LANES), F32)]
    scratch = []
    if do_cum:
        vt_spec = pl.BlockSpec((1, D_ATT // LANES, 1, LANES, tm), lambda i, m: (i, 0, m, 0, 0))
        vt_shape = jax.ShapeDtypeStruct((g, D_ATT // LANES, nt, LANES, tm), BF16)
        out_specs += [o128, pl.BlockSpec((1, 1, N_HEADS, tm), lambda i, m: (i, m, 0, 0)), vt_spec, vt_spec]
        out_shape += [jax.ShapeDtypeStruct((g, t, LANES), F32),
                      jax.ShapeDtypeStruct((g, nt, N_HEADS, tm), F32), vt_shape, vt_shape]
        scratch = [pltpu.VMEM((8, LANES), F32)]
    return pl.pallas_call(
        functools.partial(_prep_kernel, do_cum=do_cum), grid=(g, nt),
        in_specs=in_specs, out_specs=out_specs, out_shape=out_shape, scratch_shapes=scratch,
        compiler_params=_cp("arbitrary", "arbitrary"), name="qk_prep",
    )(*([u] * 7 + consts))


_KEY_NEG_INF = -(2 ** 31) + 0x7FFFFF


def _key_to_float(key):
    bits = jnp.where(key < 0, key ^ jnp.int32(0x7FFFFFFF), key)
    return jnp.where(key <= _KEY_NEG_INF, -jnp.inf, lax.bitcast_convert_type(bits, F32))


def _kth_largest(count_ge, shape, k):
    imin = jnp.int32(-2 ** 31)

    def body(i, t):
        cand = t + lax.shift_left(jnp.int32(1), jnp.int32(31) - i)
        return jnp.where(count_ge(_key_to_float(cand)) >= k, cand, t)

    t = lax.fori_loop(0, 32, body, jnp.full(shape, imin, I32))
    return _key_to_float(t), _key_to_float(t + 1)


def _pairs_t(q_ref, qt_scr):
    upper = _iota((LANES, 1), 0) < HEAD_DIM
    for j in range(D_ATT // LANES):
        t = (q_ref[0, :, j * LANES:(j + 1) * LANES].astype(F32) * (HEAD_DIM ** -0.5)).T
        qt_scr[j] = jnp.concatenate([jnp.where(upper, t, 0.0), jnp.where(upper, 0.0, t)], axis=1).astype(BF16)


def _flash_t(qt_scr, k_ref, vt_ref, bias_of, nkb, tq, o_ref):
    upper = _iota((LANES, 1), 0) < HEAD_DIM

    def pick(x):
        return jnp.where(upper, x[:, 0:tq], x[:, tq:2 * tq])

    for j in range(D_ATT // LANES):
        sl = slice(j * LANES, (j + 1) * LANES)

        def body(kb, carry, j=j, sl=sl):
            m_old, l_old, acc = carry
            ks = k_ref[0, pl.ds(pl.multiple_of(kb * tq, tq), tq), sl]
            s = _dot(ks, qt_scr[j]) + bias_of(j, kb)
            m_new = jnp.maximum(m_old, jnp.max(s, axis=0, keepdims=True))
            p = jnp.exp(s - m_new)
            alpha = jnp.exp(m_old - m_new)
            l_new = alpha * l_old + jnp.sum(p, axis=0, keepdims=True)
            o = _dot(vt_ref[0, j, kb], p.astype(BF16))
            return m_new, l_new, pick(alpha) * acc + pick(o)

        init = (jnp.full((1, 2 * tq), NEG, F32), jnp.zeros((1, 2 * tq), F32), jnp.zeros((LANES, tq), F32))
        _, l, acc = lax.fori_loop(0, nkb, body, init)
        o_ref[0, :, sl] = (acc * pick(1.0 / l)).T.astype(BF16)


def _fox_prompt_kernel(q_ref, k_ref, vt_ref, cc_ref, cr_ref, o_ref, qt_scr, *, tq):
    qi = pl.program_id(1)
    _pairs_t(q_ref, qt_scr)
    kq = _iota((tq, 2 * tq), 0) - (_iota((tq, 2 * tq), 1) & (tq - 1))
    cq_rows = cr_ref[0, qi]

    def bias_of(j, kb):
        cc = cc_ref[0, pl.ds(pl.multiple_of(kb * tq, tq), tq), :]
        ck = jnp.concatenate([jnp.broadcast_to(cc[:, SM_FF + 2 * j + w:SM_FF + 2 * j + w + 1], (tq, tq))
                              for w in range(2)], axis=1)
        cq = jnp.concatenate([cq_rows[2 * j:2 * j + 1, :], cq_rows[2 * j + 1:2 * j + 2, :]], axis=1)
        return jnp.where(kq <= (qi - kb) * tq, cq - ck, NEG)

    _flash_t(qt_scr, k_ref, vt_ref, bias_of, qi + 1, tq, o_ref)


def _fox_prompt(q, k, vt, cumc, cumr, tq):
    b, t, _ = q.shape
    nt = t // tq
    assert tq & (tq - 1) == 0
    return pl.pallas_call(
        functools.partial(_fox_prompt_kernel, tq=tq), grid=(b, nt),
        in_specs=[pl.BlockSpec((1, tq, D_ATT), lambda i, m: (i, m, 0)),
                  pl.BlockSpec((1, t, D_ATT), lambda i, m: (i, 0, 0)),
                  pl.BlockSpec((1, D_ATT // LANES, nt, LANES, tq), lambda i, m: (i, 0, 0, 0, 0)),
                  pl.BlockSpec((1, t, LANES), lambda i, m: (i, 0, 0)),
                  pl.BlockSpec((1, nt, N_HEADS, tq), lambda i, m: (i, 0, 0, 0))],
        out_specs=pl.BlockSpec((1, tq, D_ATT), lambda i, m: (i, m, 0)),
        out_shape=jax.ShapeDtypeStruct((b, t, D_ATT), BF16),
        scratch_shapes=[pltpu.VMEM((D_ATT // LANES, LANES, 2 * tq), BF16)],
        compiler_params=_cp("arbitrary", "arbitrary"), name="fox_prompt",
    )(q, k, vt, cumc, cumr)


def _dsa_prompt_kernel(iq_ref, sm_ref, ki_ref, q_ref, k_ref, vt_ref, o_ref, sc_scr, sel_scr, qt_scr, *, tq, topk):
    qi = pl.program_id(1)
    nkb = qi + 1
    kq = _iota((tq, tq), 0) - _iota((tq, tq), 1)

    def causal(kb):
        return kq <= (qi - kb) * tq

    _pairs_t(iq_ref, qt_scr)
    w_t = (sm_ref[0] * (N_HEADS ** -0.5)).T

    def score_body(kb, _):
        ks = ki_ref[0, pl.ds(pl.multiple_of(kb * tq, tq), tq), :]
        acc = jnp.zeros((tq, tq), F32)
        for j in range(D_ATT // LANES):
            d = jnp.maximum(_dot(ks, qt_scr[j]), 0.0)
            for w in range(2):
                h = 2 * j + w
                acc = acc + w_t[SM_IW + h:SM_IW + h + 1, :] * d[:, w * tq:(w + 1) * tq]
        sc_scr[kb] = jnp.where(causal(kb), acc, -jnp.inf)
        return 0

    lax.fori_loop(0, nkb, score_body, 0)

    def count(fn):
        def body(kb, part):
            hit = jnp.where(fn(sc_scr[kb]), 1.0, 0.0)
            return part + jnp.sum(hit.reshape(tq // 8, 8, tq), axis=0)

        part = lax.fori_loop(0, nkb, body, jnp.zeros((8, tq), F32))
        return jnp.sum(part, axis=0, keepdims=True)

    kf = float(topk)
    thr, nxt = _kth_largest(lambda cand: count(lambda x: x >= cand), (1, tq), kf)
    need = kf - count(lambda x: x >= nxt)
    earlier = _tri(tq, lambda r, c: r > c)

    def sel_body(kb, run):
        blk = sc_scr[kb]
        tie = (blk >= thr) & (blk < nxt)
        tief = jnp.where(tie, 1.0, 0.0)
        before = _dot(earlier, tief.astype(BF16)) + run
        sel = ((blk >= nxt) | (tie & (before < need))) & causal(kb)
        bias = jnp.where(sel, 0.0, NEG)
        sel_scr[kb] = jnp.concatenate([bias, bias], axis=1)
        return run + jnp.sum(tief, axis=0, keepdims=True)

    lax.fori_loop(0, nkb, sel_body, jnp.zeros((1, tq), F32))

    _pairs_t(q_ref, qt_scr)
    _flash_t(qt_scr, k_ref, vt_ref, lambda j, kb: sel_scr[kb], nkb, tq, o_ref)


def _dsa_prompt(u, kid, q, k, vt, tq, topk):
    b, t, _ = q.shape
    nt = t // tq
    assert tq >= topk
    return pl.pallas_call(
        functools.partial(_dsa_prompt_kernel, tq=tq, topk=topk), grid=(b, nt),
        in_specs=[pl.BlockSpec((1, tq, BLK), lambda i, m: (i, m, BLK_IQ)),
                  pl.BlockSpec((1, tq, LANES), lambda i, m: (i, m, BLK_LAST * (BLK // LANES) + 1)),
                  pl.BlockSpec((1, t, LANES), lambda i, m: (i, 0, 0)),
                  pl.BlockSpec((1, tq, D_ATT), lambda i, m: (i, m, 0)),
                  pl.BlockSpec((1, t, D_ATT), lambda i, m: (i, 0, 0)),
                  pl.BlockSpec((1, D_ATT // LANES, nt, LANES, tq), lambda i, m: (i, 0, 0, 0, 0))],
        out_specs=pl.BlockSpec((1, tq, D_ATT), lambda i, m: (i, m, 0)),
        out_shape=jax.ShapeDtypeStruct((b, t, D_ATT), BF16),
        scratch_shapes=[pltpu.VMEM((nt, tq, tq), F32), pltpu.VMEM((nt, tq, 2 * tq), F32),
                        pltpu.VMEM((D_ATT // LANES, LANES, 2 * tq), BF16)],
        compiler_params=_cp("arbitrary", "arbitrary"), name="dsa_prompt",
    )(u, u, kid, q, k, vt)


def _fox_bias_kernel(pt_ref, nlf_ref, *refs, pg):
    lf_refs = refs[:pg]
    col_o, r_o, run_scr = refs[pg:]
    s = pl.program_id(1)
    strict = _tri(LANES, lambda r, c: r > c)

    @pl.when(s == 0)
    def _():
        nlf = nlf_ref[0]
        r_o[0] = _dot3_r(nlf, strict)
        run_scr[...] = jnp.broadcast_to(jnp.sum(nlf, axis=1, keepdims=True), run_scr.shape)

    run = run_scr[...]
    for i in range(pg):
        lf = lf_refs[i][0, 0]
        col_o[0, pg - 1 - i] = -(_dot3_r(lf, strict) + run)
        run = run + jnp.sum(lf, axis=1, keepdims=True)
    run_scr[...] = run


def _fox_bias(pt, lfc_t, layer, nlf, pg):
    b, npg = pt.shape
    nst = npg // pg

    def page(i):
        return lambda bi, s, pt: (layer, pt[bi, npg - 1 - (s * pg + i)], 0, 0)

    return pl.pallas_call(
        functools.partial(_fox_bias_kernel, pg=pg),
        grid_spec=pltpu.PrefetchScalarGridSpec(
            num_scalar_prefetch=1, grid=(b, nst),
            in_specs=[pl.BlockSpec((1, N_HEADS, LANES), lambda bi, s, pt: (bi, 0, 0))]
            + [pl.BlockSpec((1, 1, N_HEADS, LANES), page(i)) for i in range(pg)],
            out_specs=[pl.BlockSpec((1, pg, N_HEADS, LANES), lambda bi, s, pt: (bi, nst - 1 - s, 0, 0)),
                       pl.BlockSpec((1, N_HEADS, LANES), lambda bi, s, pt: (bi, 0, 0))],
            scratch_shapes=[pltpu.VMEM((N_HEADS, LANES), F32)]),
        out_shape=[jax.ShapeDtypeStruct((b, npg, N_HEADS, LANES), F32),
                   jax.ShapeDtypeStruct((b, N_HEADS, LANES), F32)],
        compiler_params=_cp("arbitrary", "arbitrary"), name="fox_bias",
    )(pt, nlf, *([lfc_t] * pg))


def _index_scores(dots, w, t_new):
    d = jnp.maximum(dots, 0.0) * w
    return jnp.sum(d.reshape(t_new, N_HEADS, d.shape[1]), axis=1)


def _idx_score_kernel(pt_ref, iq_ref, w_ref, *refs, pg, t_new):
    kit_refs = refs[:pg]
    o_ref = refs[pg]
    iq = (iq_ref[0] * (HEAD_DIM ** -0.5)).astype(BF16)
    w = w_ref[0] * (N_HEADS ** -0.5)
    pad = jnp.full((8 - t_new, LANES), -jnp.inf, F32)
    for i in range(pg):
        sc = _index_scores(_dot(iq, kit_refs[i][0, 0].astype(BF16)), w, t_new)
        o_ref[0, i] = jnp.concatenate([sc, pad], axis=0)


def _idx_scores(pt, kic_t, layer, iq, w, t_new, pg):
    b, npg = pt.shape
    rows = t_new * N_HEADS

    def page(i):
        return lambda bi, s, pt: (layer, pt[bi, s * pg + i], 0, 0)

    return pl.pallas_call(
        functools.partial(_idx_score_kernel, pg=pg, t_new=t_new),
        grid_spec=pltpu.PrefetchScalarGridSpec(
            num_scalar_prefetch=1, grid=(b, npg // pg),
            in_specs=[pl.BlockSpec((1, rows, HEAD_DIM), lambda bi, s, pt: (bi, 0, 0)),
                      pl.BlockSpec((1, rows, 1), lambda bi, s, pt: (bi, 0, 0))]
            + [pl.BlockSpec((1, 1, HEAD_DIM, LANES), page(i)) for i in range(pg)],
            out_specs=pl.BlockSpec((1, pg, 8, LANES), lambda bi, s, pt: (bi, s, 0, 0))),
        out_shape=jax.ShapeDtypeStruct((b, npg, 8, LANES), F32),
        compiler_params=_cp("arbitrary", "arbitrary"), name="dsa_idx_scores",
    )(pt, iq, w, *([kic_t] * pg))


def _select_kernel(sc_ref, iq_ref, w_ref, kin_ref, mf_o, mn_o, sc_scr, cnt_scr, *, t_new, topk):
    npg = sc_ref.shape[1]
    n1 = npg + 1
    sc_scr[0:npg] = sc_ref[0]
    iq = (iq_ref[0] * (HEAD_DIM ** -0.5)).astype(BF16)
    w = w_ref[0] * (N_HEADS ** -0.5)
    scn = _index_scores(_dot_nt(iq, kin_ref[0].astype(BF16)), w, t_new)
    scn = jnp.where(_iota(scn.shape, 1) <= _iota(scn.shape, 0), scn, -jnp.inf)
    sc_scr[npg] = jnp.concatenate([scn, jnp.full((8 - t_new, LANES), -jnp.inf, F32)], axis=0)
    sc = sc_scr[...]

    def count(pred):
        part = jnp.sum(jnp.where(pred, 1.0, 0.0), axis=0)
        return jnp.sum(part, axis=1, keepdims=True)

    kf = float(topk)
    thr, nxt = _kth_largest(lambda cand: count(sc >= cand), (8, 1), kf)
    need = kf - count(sc >= nxt)
    tie = (sc >= thr) & (sc < nxt)
    tief = jnp.where(tie, 1.0, 0.0)
    strict = _tri(LANES, lambda r, c: r < c)
    before = _dot(tief.reshape(n1 * 8, LANES).astype(BF16), strict).reshape(n1, 8, LANES)
    cnt_scr[...] = jnp.sum(tief, axis=2, keepdims=True)

    def body(p, run):
        c = cnt_scr[p]
        cnt_scr[p] = run
        return run + c

    lax.fori_loop(0, n1, body, jnp.zeros((8, 1), F32))
    sel = ((sc >= nxt) | (tie & (before + cnt_scr[...] < need))) & (sc > -jnp.inf)
    memb = jnp.where(sel, 1.0, 0.0)
    mf_o[0] = memb[0:npg]
    mn_o[0] = memb[npg]


def _dsa_select(scores, iq, w, ki_new, t_new, topk):
    b, npg, _, _ = scores.shape
    rows = t_new * N_HEADS
    return pl.pallas_call(
        functools.partial(_select_kernel, t_new=t_new, topk=topk), grid=(b,),
        in_specs=[pl.BlockSpec((1, npg, 8, LANES), lambda i: (i, 0, 0, 0)),
                  pl.BlockSpec((1, rows, HEAD_DIM), lambda i: (i, 0, 0)),
                  pl.BlockSpec((1, rows, 1), lambda i: (i, 0, 0)),
                  pl.BlockSpec((1, LANES, HEAD_DIM), lambda i: (i, 0, 0))],
        out_specs=[pl.BlockSpec((1, npg, 8, LANES), lambda i: (i, 0, 0, 0)),
                   pl.BlockSpec((1, 8, LANES), lambda i: (i, 0, 0))],
        out_shape=[jax.ShapeDtypeStruct((b, npg, 8, LANES), F32),
                   jax.ShapeDtypeStruct((b, 8, LANES), F32)],
        scratch_shapes=[pltpu.VMEM((npg + 1, 8, LANES), F32), pltpu.VMEM((npg + 1, 8, 1), F32)],
        compiler_params=_cp("arbitrary"), name="dsa_select",
    )(scores, iq, w, ki_new)


def _decode_kernel(pt_ref, q_ref, kn_ref, vn_ref, *refs, pg, nsteps, t_new, fox):
    kt_refs = refs[:pg]
    vt_refs = refs[pg:2 * pg]
    if fox:
        row_ref, col_ref, coln_ref, o_ref, qbd_scr, m_scr, l_scr, acc_scr = refs[2 * pg:]
    else:
        msk_ref, mskn_ref, o_ref, qbd_scr, m_scr, l_scr, acc_scr = refs[2 * pg:]
    s = pl.program_id(1)
    rows = t_new * N_HEADS
    lane = _iota((N_HEADS, D_ATT), 1)
    hrow = _iota((N_HEADS, D_ATT), 0)
    hmask = (lane >= hrow * HEAD_DIM) & (lane < (hrow + 1) * HEAD_DIM)

    def per_query(me):
        return jnp.concatenate([jnp.broadcast_to(me[t:t + 1, :], (N_HEADS, LANES)) for t in range(t_new)], axis=0)

    def per_head(me):
        return jnp.concatenate([me] * t_new, axis=0)

    @pl.when(s == 0)
    def _():
        q = q_ref[0].astype(F32) * (HEAD_DIM ** -0.5)
        for t in range(t_new):
            qt = jnp.broadcast_to(q[t:t + 1, :], (N_HEADS, D_ATT))
            qbd_scr[t * N_HEADS:(t + 1) * N_HEADS, :] = jnp.where(hmask, qt, 0.0).astype(BF16)
        sn = _dot_nt(qbd_scr[...], kn_ref[0])
        if fox:
            causal = _iota((rows, LANES), 1) <= lax.shift_right_logical(_iota((rows, LANES), 0), 3)
            sn = sn + jnp.where(causal, row_ref[0] - per_head(coln_ref[0]), NEG)
        else:
            sn = sn + jnp.where(per_query(mskn_ref[0]) > 0.5, 0.0, NEG)
        m = jnp.max(sn, axis=1, keepdims=True)
        p = jnp.exp(sn - m)
        m_scr[...] = m
        l_scr[...] = jnp.sum(p, axis=1, keepdims=True)
        acc_scr[...] = _dot(p.astype(BF16), vn_ref[0])

    qbd = qbd_scr[...]
    ss = []
    for i in range(pg):
        if fox:
            bias = row_ref[0] - per_head(col_ref[0, i])
        else:
            bias = jnp.where(per_query(msk_ref[0, i]) > 0.5, 0.0, NEG)
        ss.append(_dot(qbd, kt_refs[i][0, 0].astype(BF16)) + bias)
    m_old = m_scr[...]
    m_new = m_old
    for x in ss:
        m_new = jnp.maximum(m_new, jnp.max(x, axis=1, keepdims=True))
    alpha = jnp.exp(m_old - m_new)
    l = alpha * l_scr[...]
    acc = alpha * acc_scr[...]
    for i, x in enumerate(ss):
        p = jnp.exp(x - m_new)
        l = l + jnp.sum(p, axis=1, keepdims=True)
        acc = acc + _dot_nt(p.astype(BF16), vt_refs[i][0, 0].astype(BF16))
    m_scr[...] = m_new
    l_scr[...] = l
    acc_scr[...] = acc

    @pl.when(s == nsteps - 1)
    def _():
        out = acc / l
        ys = []
        for t in range(t_new):
            blk = jnp.where(hmask, out[t * N_HEADS:(t + 1) * N_HEADS, :], 0.0)
            ys.append(jnp.sum(blk, axis=0, keepdims=True))
        o_ref[0] = jnp.concatenate(ys, axis=0).astype(BF16)


def _decode_attn(pt, q, kc_t, vc_t, layer, k_new, v_new, bias_args, t_new, pg, fox):
    b, npg = pt.shape
    rows = t_new * N_HEADS
    nst = npg // pg

    def page(i):
        return lambda bi, s, pt: (layer, pt[bi, s * pg + i], 0, 0)

    def per_b(shape):
        return pl.BlockSpec((1,) + shape, lambda bi, s, pt: (bi,) + (0,) * len(shape))

    cache_specs = [pl.BlockSpec((1, 1, D_ATT, LANES), page(i)) for i in range(pg)]
    paged = pl.BlockSpec((1, pg, 8, LANES), lambda bi, s, pt: (bi, s, 0, 0))
    bias_specs = ([per_b((rows, 1)), paged, per_b((8, LANES))] if fox else [paged, per_b((8, LANES))])
    return pl.pallas_call(
        functools.partial(_decode_kernel, pg=pg, nsteps=nst, t_new=t_new, fox=fox),
        grid_spec=pltpu.PrefetchScalarGridSpec(
            num_scalar_prefetch=1, grid=(b, nst),
            in_specs=[per_b((t_new, D_ATT)), per_b((LANES, D_ATT)), per_b((LANES, D_ATT))]
            + cache_specs + cache_specs + bias_specs,
            out_specs=per_b((t_new, D_ATT)),
            scratch_shapes=[pltpu.VMEM((rows, D_ATT), BF16), pltpu.VMEM((rows, 1), F32),
                            pltpu.VMEM((rows, 1), F32), pltpu.VMEM((rows, D_ATT), F32)]),
        out_shape=jax.ShapeDtypeStruct((b, t_new, D_ATT), BF16),
        compiler_params=_cp("arbitrary", "arbitrary"), name="decode_attn",
    )(pt, q, k_new, v_new, *([kc_t] * pg), *([vc_t] * pg), *bias_args)


def _pack_w_in(w_in):
    o = 0
    seg = {}
    for name, n in (('z', 1024), ('xbc', 1536), ('dt', 16), ('fq', 512), ('fk', 512), ('fv', 512), ('ff', 8),
                    ('dq', 512), ('dk', 512), ('dv', 512), ('iq', 512), ('ik', 64), ('iw', 8)):
        seg[name] = w_in[:, :, o:o + n]
        o += n
    used = 1024 + 1536 + 7 * 512 + 64 + 64 + 16 + 8 + 8
    pad = jnp.zeros(w_in.shape[:2] + (N_PACK - used,), w_in.dtype)
    parts = [seg[k] for k in ('z', 'xbc', 'fq', 'fk', 'fv', 'dq', 'dk', 'dv', 'iq', 'ik', 'ik', 'dt', 'ff', 'iw')]
    return jnp.concatenate(parts + [pad], axis=-1).astype(BF16)


def _lane_row(vals, offset):
    return jnp.zeros((1, LANES), F32).at[0, offset:offset + vals.shape[0]].set(vals.astype(F32))


def _layer_consts(l, P):
    a = -jnp.exp(P['ssd_a_log'][l].astype(F32))
    dtb_row = _lane_row(P['ssd_dt_bias'][l], SM_DT)
    a_row = _lane_row(a, SM_DT)
    gmat = jnp.kron(jnp.eye(N_HEADS, dtype=F32), jnp.full((HEAD_DIM, HEAD_DIM), 1.0 / HEAD_DIM, F32)).astype(BF16)

    def tile_heads(g):
        return jnp.tile(g.astype(F32), N_HEADS).reshape(1, D_ATT)

    return dict(
        ssd_conv_w=P['ssd_conv_w'][l], ssd_conv_b=P['ssd_conv_b'][l].reshape(1, -1),
        dtb_row=dtb_row, dtb_col=dtb_row.reshape(LANES, 1), a_row=a_row, a_col=a_row.reshape(LANES, 1),
        d_exp=jnp.repeat(P['ssd_d'][l].astype(F32), HEAD_DIM).reshape(1, D_SSD),
        ssd_norm=P['ssd_norm'][l].reshape(1, D_SSD), gmat=gmat,
        fox_q_norm=tile_heads(P['fox_q_norm'][l]), fox_k_norm=tile_heads(P['fox_k_norm'][l]),
        dsa_q_norm=tile_heads(P['dsa_q_norm'][l]), dsa_k_norm=tile_heads(P['dsa_k_norm'][l]),
        fb_row=_lane_row(P['fox_f_bias'][l], SM_FF),
    )


def _pad_rows(x, rows):
    return jnp.pad(x, ((0, 0), (0, rows - x.shape[1]), (0, 0)))


def _layer(l, x, mod, P, wl, past):
    b, t, d = x.shape
    sh1, sc1, g1, sh2, sc2, g2 = [mod[:, None, i * d:(i + 1) * d] for i in range(6)]
    prompt = past is None
    m = b * t
    f2 = P['w_up'].shape[-1]

    h = _norm_mod(x, P['attn_norm'][l], sc1, sh1)
    if prompt:
        u = _matmul(h, P['w_in_p'][l], 512, TN_IN)
        uf = u
    else:
        uf = _matmul(h.reshape(1, m, d), P['w_in_p'][l], m, TN_IN)
        u = uf.reshape(b, t, N_PACK)

    if prompt:
        y_ssd, st = _ssd(u, wl, SSD_CHUNK)
    else:
        prev = jnp.pad(past['state_ssd_conv'][l], ((0, 0), (8 - (SSD_CONV - 1), 0), (0, 0)))
        h0 = past['state_ssd'][l].reshape(b, SSD_NHEADS * HEAD_DIM, SSD_STATE)
        y_ssd, st = _ssd(_pad_rows(u, SSD_CHUNK), wl, t, prev, h0)
        y_ssd = y_ssd[:, :t]
    ssd_new = st.reshape(b, SSD_NHEADS, HEAD_DIM, SSD_STATE)
    xbc_raw = u[:, :, BLK_Z * BLK + D_SSD:BLK_Z * BLK + D_SSD + 3 * BLK]
    if prompt:
        ssd_conv_new = xbc_raw[:, t - (SSD_CONV - 1):]
    else:
        ssd_conv_new = jnp.concatenate([past['state_ssd_conv'][l], xbc_raw], axis=1)[:, t:]

    tq = 256
    (fqn, fkn, fkb, fvb, dqn, dkn, dkb, dvb, kid, logf, *cum) = _prep(uf, wl, tq, prompt)
    fk_out = fkn.reshape(b, t, N_HEADS, HEAD_DIM)
    dk_out = dkn.reshape(b, t, N_HEADS, HEAD_DIM)
    fv_out = u[:, :, BLK_FV * BLK:(BLK_FV + 1) * BLK].reshape(b, t, N_HEADS, HEAD_DIM)
    dv_out = u[:, :, BLK_DV * BLK:(BLK_DV + 1) * BLK].reshape(b, t, N_HEADS, HEAD_DIM)
    ik_out = u[:, :, BLK_LAST * BLK:BLK_LAST * BLK + HEAD_DIM]
    logf_out = logf.reshape(b, t, LANES)[:, :, SM_FF:SM_FF + N_HEADS]

    if prompt:
        cumc, cumr, fvt, dvt = cum
        y_fox = _fox_prompt(fqn, fkb, fvt, cumc, cumr, tq)
        y_dsa = _dsa_prompt(u, kid, dqn, dkb, dvt, tq, min(TOPK_MAX, t // 4))
    else:
        pt = past['page_table']
        npg = pt.shape[1]
        page = past['kidx_t'].shape[3]
        assert page == LANES
        rows = t * N_HEADS
        pg_kv = min(PAGES_PER_STEP_KV, npg)
        pg_aux = min(PAGES_PER_STEP_AUX, npg)

        def r3(a):
            return a.reshape(b, t, -1)

        def new_page(a):
            return _pad_rows(r3(a), LANES)

        nlf = jnp.pad(jnp.swapaxes(logf_out, 1, 2), ((0, 0), (0, 0), (0, LANES - t)))
        col, rnew = _fox_bias(pt, past['fox_logf_t'], l, nlf, pg_aux)
        row_term = jnp.swapaxes(-rnew[:, :, :t], 1, 2).reshape(b, rows, 1)
        y_fox = _decode_attn(pt, r3(fqn), past['fox_kt'], past['fox_vt'], l, new_page(fkb), new_page(fvb),
                             (row_term, col, -rnew), t, pg_kv, True)

        iq = u[:, :, BLK_IQ * BLK:(BLK_IQ + 1) * BLK].reshape(b, rows, HEAD_DIM)
        iw = u[:, :, BLK_LAST * BLK + LANES + SM_IW:BLK_LAST * BLK + LANES + SM_IW + N_HEADS].reshape(b, rows, 1)
        sc = _idx_scores(pt, past['kidx_t'], l, iq, iw, t, pg_aux)
        memb, memb_new = _dsa_select(sc, iq, iw, _pad_rows(ik_out, LANES), t,
                                     min(TOPK_MAX, (npg * page + t) // 4))
        y_dsa = _decode_attn(pt, r3(dqn), past['dsa_kt'], past['dsa_vt'], l, new_page(dkb), new_page(dvb),
                             (memb, memb_new), t, pg_kv, False)

    mixed = (y_ssd, y_fox.reshape(b, t, D_ATT), y_dsa.reshape(b, t, D_ATT))
    if prompt:
        x = _out_proj(mixed, P['w_out_b'][l], 512, x, g1)
    else:
        x = _out_proj([a.reshape(1, m, -1) for a in mixed], P['w_out_b'][l], m, x.reshape(1, m, d),
                      jnp.broadcast_to(g1, (b, t, d)).reshape(1, m, d)).reshape(b, t, d)

    h2 = _norm_mod(x, P['ffn_norm'][l], sc2, sh2)
    cw = P['ffn_conv_w'][l]
    cb = P['ffn_conv_b'][l].reshape(1, f2)
    if prompt:
        act, ta, tg = _ffn_up_prompt(h2.reshape(m, d), P['w_up_b'][l], cw, cb, t)
        ffn_conv_new = jnp.concatenate([ta, tg], axis=-1)[:, 8 - (FFN_CONV - 1):]
        x = _matmul(act.reshape(b, t, -1), P['w_down_b'][l], 512, TN_DOWN, res=x, gate=g2)
    else:
        buf = past['state_ffn_conv'][l]
        zero = jnp.zeros((b, t - 1, f2), F32)
        p1 = jnp.concatenate([buf[:, 1:2], zero], axis=1).reshape(m, f2)
        p2 = jnp.concatenate([buf[:, 0:2], zero[:, 1:]], axis=1).reshape(m, f2)
        act, ra, rg = _ffn_up_sample(h2.reshape(m, d), P['w_up_b'][l], cw, cb, p1, p2, t)
        raw = jnp.concatenate([ra, rg], axis=-1).reshape(b, t, f2)
        ffn_conv_new = jnp.concatenate([buf, raw], axis=1)[:, t:]
        x = _matmul(act.reshape(1, m, -1), P['w_down_b'][l], m, TN_DOWN, res=x.reshape(1, m, d),
                    gate=jnp.broadcast_to(g2, (b, t, d)).reshape(1, m, d)).reshape(b, t, d)

    new = (fk_out, fv_out, logf_out, dk_out, dv_out, ik_out, ssd_new, ssd_conv_new, ffn_conv_new)
    return x, new


def kernel(x_prompt, x_sample, cache_fox_k, cache_fox_v, cache_fox_logf, cache_dsa_k, cache_dsa_v, cache_dsa_kidx, state_ssd, state_ssd_conv, state_ffn_conv, page_table, c_prompt, c_sample, w_ada, b_ada, attn_norm, w_in, ssd_conv_w, ssd_conv_b, ssd_dt_bias, ssd_a_log, ssd_d, ssd_norm, fox_q_norm, fox_k_norm, fox_f_bias, dsa_q_norm, dsa_k_norm, w_out, ffn_norm, w_up, ffn_conv_w, ffn_conv_b, w_down):
    depth = w_in.shape[0]
    bp = x_prompt.shape[0]
    bs = x_sample.shape[0]
    P = dict(attn_norm=attn_norm, ssd_conv_w=ssd_conv_w, ssd_conv_b=ssd_conv_b, ssd_dt_bias=ssd_dt_bias,
             ssd_a_log=ssd_a_log, ssd_d=ssd_d, ssd_norm=ssd_norm, fox_q_norm=fox_q_norm, fox_k_norm=fox_k_norm,
             fox_f_bias=fox_f_bias, dsa_q_norm=dsa_q_norm, dsa_k_norm=dsa_k_norm, ffn_norm=ffn_norm,
             ffn_conv_w=ffn_conv_w, ffn_conv_b=ffn_conv_b, w_up=w_up,
             w_in_p=_pack_w_in(w_in), w_out_b=w_out.astype(BF16), w_up_b=w_up.astype(BF16),
             w_down_b=w_down.astype(BF16))
    npool, page = cache_fox_k.shape[1], cache_fox_k.shape[2]

    def positions_minor(c):
        return jnp.transpose(c, (0, 1, 3, 4, 2)).reshape(depth, npool, N_HEADS * HEAD_DIM, page)

    past = dict(state_ssd=state_ssd, state_ssd_conv=state_ssd_conv, state_ffn_conv=state_ffn_conv,
                page_table=page_table,
                fox_kt=positions_minor(cache_fox_k), fox_vt=positions_minor(cache_fox_v),
                dsa_kt=positions_minor(cache_dsa_k), dsa_vt=positions_minor(cache_dsa_v),
                kidx_t=jnp.swapaxes(cache_dsa_kidx, 2, 3),
                fox_logf_t=jnp.swapaxes(cache_fox_logf, 2, 3))

    rows = bp + bs
    rows_pad = -(-rows // 8) * 8
    c_all = jnp.pad(jnp.concatenate([c_prompt, c_sample], axis=0), ((0, rows_pad - rows), (0, 0)))
    mods = _ada_mod(c_all, w_ada, b_ada)

    xp, xs = x_prompt, x_sample
    new_p, new_s = [], []
    for l in range(depth):
        wl = _layer_consts(l, P)
        xp, sp = _layer(l, xp, mods[l, :bp], P, wl, None)
        xs, ss = _layer(l, xs, mods[l, bp:rows], P, wl, past)
        new_p.append(sp)
        new_s.append(ss)
    outs_p = [jnp.stack(a, 0) for a in zip(*new_p)]
    outs_s = [jnp.stack(a, 0) for a in zip(*new_s)]
    return tuple([xp, xs] + outs_p + outs_s)
```

```python
import functools

import jax
import jax.numpy as jnp
from jax import lax
from jax.experimental import pallas as pl
from jax.experimental.pallas import tpu as pltpu

F32 = jnp.float32
BF16 = jnp.bfloat16
I32 = jnp.int32

RMS_EPS = 1e-6
NEG = -1e30
HEAD_DIM = 64
LANES = 128
D_SSD = 1024
SSD_STATE = 128
SSD_NHEADS = 16
SSD_CONV = 4
SSD_CHUNK = 128
D_ATT = 512
N_HEADS = 8
FFN_CONV = 3
TOPK_MAX = 256
VMEM_LIMIT = 56 * 1024 * 1024
PAGES_PER_STEP_KV = 16
PAGES_PER_STEP_AUX = 16

BLK = 512
BLK_Z, BLK_X, BLK_FQ, BLK_FK, BLK_FV, BLK_DQ, BLK_DK, BLK_DV, BLK_IQ, BLK_LAST = 0, 2, 5, 6, 7, 8, 9, 10, 11, 12
N_PACK = 13 * BLK
TN_IN = N_PACK // 4
TN_DOWN = 1024
SM_DT, SM_FF, SM_IW = 0, 16, 24


def _cp(*sem):
    return pltpu.CompilerParams(dimension_semantics=sem, vmem_limit_bytes=VMEM_LIMIT)


def _dot(a, b):
    return jnp.dot(a, b, preferred_element_type=F32)


def _dot_nt(a, b):
    return lax.dot_general(a, b, (((1,), (1,)), ((), ())), preferred_element_type=F32)


def _split3(x):
    hi = x.astype(BF16)
    r = x - hi.astype(F32)
    mid = r.astype(BF16)
    lo = (r - mid.astype(F32)).astype(BF16)
    return hi, mid, lo


def _dot3_l(m01, x):
    hi, mid, lo = _split3(x)
    return _dot(m01, hi) + _dot(m01, mid) + _dot(m01, lo)


def _dot3_r(x, m01):
    hi, mid, lo = _split3(x)
    return _dot(hi, m01) + _dot(mid, m01) + _dot(lo, m01)


def _softplus(x):
    return jnp.maximum(x, 0.0) + jnp.log1p(jnp.exp(-jnp.abs(x)))


def _silu(x):
    return x * jax.nn.sigmoid(x)


def _iota(shape, dim):
    return lax.broadcasted_iota(I32, shape, dim)


def _tri(n, fn):
    r = _iota((n, n), 0)
    c = _iota((n, n), 1)
    return jnp.where(fn(r, c), 1.0, 0.0).astype(BF16)


def _ada_kernel(c_ref, w_ref, b_ref, o_ref):
    s = _silu(c_ref[...]).astype(BF16)
    o_ref[0] = _dot(s, w_ref[0].astype(BF16)) + b_ref[0]


def _ada_mod(c_all, w_ada, b_ada):
    nl, d, n = w_ada.shape
    rows = c_all.shape[0]
    tn = 1024
    return pl.pallas_call(
        _ada_kernel, grid=(nl, n // tn),
        in_specs=[pl.BlockSpec((rows, d), lambda l, j: (0, 0)),
                  pl.BlockSpec((1, d, tn), lambda l, j: (l, 0, j)),
                  pl.BlockSpec((1, 1, tn), lambda l, j: (l, 0, j))],
        out_specs=pl.BlockSpec((1, rows, tn), lambda l, j: (l, 0, j)),
        out_shape=jax.ShapeDtypeStruct((nl, rows, n), F32),
        compiler_params=_cp("arbitrary", "arbitrary"), name="ada_mod",
    )(c_all, w_ada, b_ada.reshape(nl, 1, n))


def _normmod_kernel(x_ref, g_ref, sc_ref, sh_ref, o_ref):
    x = x_ref[0]
    ms = jnp.mean(x * x, axis=-1, keepdims=True)
    y = x * lax.rsqrt(ms + RMS_EPS) * g_ref[...]
    o_ref[0] = (y * (1.0 + sc_ref[0]) + sh_ref[0]).astype(BF16)


def _norm_mod(x, g, sc, sh):
    b, t, d = x.shape
    tm = min(t, 512)
    return pl.pallas_call(
        _normmod_kernel, grid=(b, t // tm),
        in_specs=[pl.BlockSpec((1, tm, d), lambda i, m: (i, m, 0)),
                  pl.BlockSpec((1, d), lambda i, m: (0, 0)),
                  pl.BlockSpec((1, 1, d), lambda i, m: (i, 0, 0)),
                  pl.BlockSpec((1, 1, d), lambda i, m: (i, 0, 0))],
        out_specs=pl.BlockSpec((1, tm, d), lambda i, m: (i, m, 0)),
        out_shape=jax.ShapeDtypeStruct((b, t, d), BF16),
        compiler_params=_cp("arbitrary", "arbitrary"), name="norm_mod",
    )(x, g.reshape(1, d), sc, sh)


def _mm_kernel(x_ref, w_ref, o_ref):
    o_ref[0] = _dot(x_ref[0], w_ref[...])


def _mm_res_kernel(x_ref, w_ref, r_ref, g_ref, o_ref):
    o_ref[0] = r_ref[0] + g_ref[0] * _dot(x_ref[0], w_ref[...])


def _mm3_res_kernel(x1_ref, x2_ref, x3_ref, w1_ref, w2_ref, w3_ref, r_ref, g_ref, o_ref):
    acc = _dot(x1_ref[0], w1_ref[...]) + _dot(x2_ref[0], w2_ref[...]) + _dot(x3_ref[0], w3_ref[...])
    o_ref[0] = r_ref[0] + g_ref[0] * acc


def _out_proj(xs, w, tm, res, gate):
    g, t, _ = xs[0].shape
    n = w.shape[1]
    tm = min(tm, t)
    widths = [x.shape[2] for x in xs]
    unit = widths[1]
    assert widths[2] == unit and widths[0] % unit == 0
    starts = [0, widths[0] // unit, widths[0] // unit + 1]

    def x_spec(k):
        return pl.BlockSpec((1, tm, k), lambda i, m: (i, m, 0))

    def w_spec(k, s):
        return pl.BlockSpec((k, n), lambda i, m: (s if k == unit else 0, 0))

    o_spec = pl.BlockSpec((1, tm, n), lambda i, m: (i, m, 0))
    if gate.shape[1] == 1:
        g_spec = pl.BlockSpec((1, 1, n), lambda i, m: (i, 0, 0))
    else:
        g_spec = o_spec
    return pl.pallas_call(
        _mm3_res_kernel, grid=(g, t // tm),
        in_specs=[x_spec(k) for k in widths] + [w_spec(k, s) for k, s in zip(widths, starts)] + [o_spec, g_spec],
        out_specs=o_spec, out_shape=jax.ShapeDtypeStruct((g, t, n), F32),
        compiler_params=_cp("arbitrary", "arbitrary"), name="out_proj",
    )(*xs, w, w, w, res, gate)


def _matmul(x, w, tm, tn, res=None, gate=None):
    g, t, k = x.shape
    n = w.shape[1]
    tm = min(tm, t)
    grid = (n // tn, g, t // tm)
    x_spec = pl.BlockSpec((1, tm, k), lambda j, i, m: (i, m, 0))
    w_spec = pl.BlockSpec((k, tn), lambda j, i, m: (0, j))
    o_spec = pl.BlockSpec((1, tm, tn), lambda j, i, m: (i, m, j))
    out_shape = jax.ShapeDtypeStruct((g, t, n), F32)
    cp = _cp("arbitrary", "arbitrary", "arbitrary")
    if res is None:
        return pl.pallas_call(_mm_kernel, grid=grid, in_specs=[x_spec, w_spec], out_specs=o_spec,
                              out_shape=out_shape, compiler_params=cp, name="proj")(x, w)
    if gate.shape[1] == 1:
        g_spec = pl.BlockSpec((1, 1, tn), lambda j, i, m: (i, 0, j))
    else:
        g_spec = pl.BlockSpec((1, tm, tn), lambda j, i, m: (i, m, j))
    return pl.pallas_call(_mm_res_kernel, grid=grid, in_specs=[x_spec, w_spec, o_spec, g_spec],
                          out_specs=o_spec, out_shape=out_shape, compiler_params=cp,
                          name="proj_res")(x, w, res, gate)


def _conv3(u, prev1, prev2, cw_ref, cb_ref):
    return cb_ref[...] + prev2 * cw_ref[0:1, :] + prev1 * cw_ref[1:2, :] + u * cw_ref[2:3, :]


def _shift_with_carry(u, carry):
    r1 = pltpu.roll(u, 1, 0)
    r2 = pltpu.roll(u, 2, 0)
    c1 = pltpu.roll(carry, 1, 0)
    c2 = pltpu.roll(carry, 2, 0)
    row = _iota((8, u.shape[1]), 0)
    p1 = jnp.concatenate([jnp.where(row < 1, c1, r1[0:8]), r1[8:]], axis=0)
    p2 = jnp.concatenate([jnp.where(row < 2, c2, r2[0:8]), r2[8:]], axis=0)
    return p1, p2


def _ffn_up_prompt_kernel(x_ref, wa_ref, wg_ref, cwa_ref, cwg_ref, cba_ref, cbg_ref,
                          act_ref, ta_ref, tg_ref, ca_scr, cg_scr, *, tiles_per_seq):
    m = pl.program_id(1)
    tm = x_ref.shape[0]

    @pl.when(m % tiles_per_seq == 0)
    def _():
        ca_scr[...] = jnp.zeros_like(ca_scr)
        cg_scr[...] = jnp.zeros_like(cg_scr)

    x = x_ref[...]
    ua = _dot(x, wa_ref[...])
    ug = _dot(x, wg_ref[...])
    pa1, pa2 = _shift_with_carry(ua, ca_scr[...])
    pg1, pg2 = _shift_with_carry(ug, cg_scr[...])
    a = _conv3(ua, pa1, pa2, cwa_ref, cba_ref)
    gt = _conv3(ug, pg1, pg2, cwg_ref, cbg_ref)
    act_ref[...] = (_silu(gt) * a).astype(BF16)
    ca_scr[...] = ua[tm - 8:tm]
    cg_scr[...] = ug[tm - 8:tm]
    ta_ref[0] = ua[tm - 8:tm]
    tg_ref[0] = ug[tm - 8:tm]


def _ffn_up_prompt(h, w_up, cw, cb, t_seq):
    m, k = h.shape
    f = w_up.shape[1] // 2
    tm, tn = 512, 512
    nj = f // tn
    nb = m // t_seq
    tps = t_seq // tm
    cw_spec_a = pl.BlockSpec((FFN_CONV, tn), lambda j, i: (0, j))
    cw_spec_g = pl.BlockSpec((FFN_CONV, tn), lambda j, i: (0, j + nj))
    cb_spec_a = pl.BlockSpec((1, tn), lambda j, i: (0, j))
    cb_spec_g = pl.BlockSpec((1, tn), lambda j, i: (0, j + nj))
    tail_spec = pl.BlockSpec((1, 8, tn), lambda j, i: (i // tps, 0, j))
    return pl.pallas_call(
        functools.partial(_ffn_up_prompt_kernel, tiles_per_seq=tps),
        grid=(nj, m // tm),
        in_specs=[pl.BlockSpec((tm, k), lambda j, i: (i, 0)),
                  pl.BlockSpec((k, tn), lambda j, i: (0, j)),
                  pl.BlockSpec((k, tn), lambda j, i: (0, j + nj)),
                  cw_spec_a, cw_spec_g, cb_spec_a, cb_spec_g],
        out_specs=[pl.BlockSpec((tm, tn), lambda j, i: (i, j)), tail_spec, tail_spec],
        out_shape=[jax.ShapeDtypeStruct((m, f), BF16),
                   jax.ShapeDtypeStruct((nb, 8, f), F32),
                   jax.ShapeDtypeStruct((nb, 8, f), F32)],
        scratch_shapes=[pltpu.VMEM((8, tn), F32), pltpu.VMEM((8, tn), F32)],
        compiler_params=_cp("arbitrary", "arbitrary"), name="ffn_up_prompt",
    )(h, w_up, w_up, cw, cw, cb, cb)


def _ffn_up_sample_kernel(x_ref, wa_ref, wg_ref, cwa_ref, cwg_ref, cba_ref, cbg_ref,
                          p1a_ref, p2a_ref, p1g_ref, p2g_ref, act_ref, ra_ref, rg_ref, *, t_seq):
    x = x_ref[...]
    ua = _dot(x, wa_ref[...])
    ug = _dot(x, wg_ref[...])
    t = _iota(ua.shape, 0) % t_seq

    def prevs(u, p1_ref, p2_ref):
        p1 = jnp.where(t >= 1, pltpu.roll(u, 1, 0), 0.0) + p1_ref[...]
        p2 = jnp.where(t >= 2, pltpu.roll(u, 2, 0), 0.0) + p2_ref[...]
        return p1, p2

    pa1, pa2 = prevs(ua, p1a_ref, p2a_ref)
    pg1, pg2 = prevs(ug, p1g_ref, p2g_ref)
    a = _conv3(ua, pa1, pa2, cwa_ref, cba_ref)
    gt = _conv3(ug, pg1, pg2, cwg_ref, cbg_ref)
    act_ref[...] = (_silu(gt) * a).astype(BF16)
    ra_ref[...] = ua
    rg_ref[...] = ug


def _ffn_up_sample(h, w_up, cw, cb, p1, p2, t_seq):
    m, k = h.shape
    f = w_up.shape[1] // 2
    tn = 512
    nj = f // tn
    a_spec = pl.BlockSpec((m, tn), lambda j: (0, j))
    g_spec = pl.BlockSpec((m, tn), lambda j: (0, j + nj))
    return pl.pallas_call(
        functools.partial(_ffn_up_sample_kernel, t_seq=t_seq),
        grid=(nj,),
        in_specs=[pl.BlockSpec((m, k), lambda j: (0, 0)),
                  pl.BlockSpec((k, tn), lambda j: (0, j)),
                  pl.BlockSpec((k, tn), lambda j: (0, j + nj)),
                  pl.BlockSpec((FFN_CONV, tn), lambda j: (0, j)),
                  pl.BlockSpec((FFN_CONV, tn), lambda j: (0, j + nj)),
                  pl.BlockSpec((1, tn), lambda j: (0, j)),
                  pl.BlockSpec((1, tn), lambda j: (0, j + nj)),
                  a_spec, a_spec, g_spec, g_spec],
        out_specs=[a_spec, a_spec, a_spec],
        out_shape=[jax.ShapeDtypeStruct((m, f), BF16),
                   jax.ShapeDtypeStruct((m, f), F32),
                   jax.ShapeDtypeStruct((m, f), F32)],
        compiler_params=_cp("arbitrary"), name="ffn_up_sample",
    )(h, w_up, w_up, cw, cw, cb, cb, p1, p2, p1, p2)


def _ssd_kernel(*refs, valid, has_init, nchunks):
    if has_init:
        (z0, z1, x0, x1, x2, sm_ref, cw_ref, cb_ref, dtb_row, dtb_col, a_row, a_col, d_ref, ng_ref,
         prev_ref, h0_ref, y_ref, st_ref, xx_scr, st_scr) = refs
    else:
        (z0, z1, x0, x1, x2, sm_ref, cw_ref, cb_ref, dtb_row, dtb_col, a_row, a_col, d_ref, ng_ref,
         y_ref, st_ref, xx_scr, st_scr) = refs
    c = pl.program_id(1)
    L = SSD_CHUNK
    npair = SSD_NHEADS // 2

    @pl.when(c == 0)
    def _():
        if has_init:
            xx_scr[0:8, :] = prev_ref[0]
            for j in range(npair):
                st_scr[j] = h0_ref[0, j * L:(j + 1) * L, :].T
        else:
            xx_scr[0:8, :] = jnp.zeros((8, xx_scr.shape[1]), F32)
            st_scr[...] = jnp.zeros_like(st_scr)

    xx_scr[8:8 + L, 0:BLK] = x0[0]
    xx_scr[8:8 + L, BLK:2 * BLK] = x1[0]
    xx_scr[8:8 + L, 2 * BLK:3 * BLK] = x2[0]
    acc = cb_ref[...] + xx_scr[5:5 + L, :] * cw_ref[0:1, :]
    for i in range(1, SSD_CONV):
        acc = acc + xx_scr[5 + i:5 + i + L, :] * cw_ref[i:i + 1, :]
    xx_scr[0:8, :] = xx_scr[L:L + 8, :]
    xbc = _silu(acc)
    xs = xbc[:, 0:D_SSD]
    bm = xbc[:, D_SSD:D_SSD + 2 * SSD_STATE]
    cm = xbc[:, D_SSD + 2 * SSD_STATE:D_SSD + 4 * SSD_STATE]

    small = sm_ref[0]
    dt_c = _softplus(small + dtb_row[...])
    small_t = small.T
    dt_r = _softplus(small_t + dtb_col[...])
    if valid < L:
        dt_c = jnp.where(_iota((L, LANES), 0) < valid, dt_c, 0.0)
        dt_r = jnp.where(_iota((LANES, L), 1) < valid, dt_r, 0.0)
    a_c = dt_c * a_row[...]
    a_r = dt_r * a_col[...]
    tril = _tri(L, lambda r, cc: r >= cc)
    triu = _tri(L, lambda r, cc: r <= cc)
    ones = jnp.ones((L, L), BF16)
    cs_c = _dot3_l(tril, a_c)
    cs_r = _dot3_r(a_r, triu)
    tot_c = _dot3_l(ones, a_c)
    e_cs = jnp.exp(cs_c)
    e_dec = jnp.exp(tot_c - cs_c)
    e_tot = jnp.exp(tot_c)

    lane = _iota((L, LANES), 1)
    first = lane < HEAD_DIM
    causal = _iota((L, L), 0) >= _iota((L, L), 1)

    def bcast(mat, h):
        return jnp.broadcast_to(mat[:, h:h + 1], (L, LANES))

    def pair(mat, h):
        return jnp.where(first, bcast(mat, h), bcast(mat, h + 1))

    ys = []
    for g in range(2):
        b_g = bm[:, g * SSD_STATE:(g + 1) * SSD_STATE]
        c_g = cm[:, g * SSD_STATE:(g + 1) * SSD_STATE].astype(BF16)
        cb_g = _dot_nt(c_g, b_g.astype(BF16))
        bt_g = b_g.T.astype(BF16)
        for jj in range(npair // 2):
            j = g * (npair // 2) + jj
            ha = 2 * j
            xs_p = xs[:, j * LANES:(j + 1) * LANES]
            x_p = xs_p * pair(dt_c, ha)
            x_pb = x_p.astype(BF16)
            yd = []
            for h in (ha, ha + 1):
                diff = jnp.minimum(bcast(cs_c, h) - cs_r[h:h + 1, :], 0.0)
                lm = jnp.where(causal, jnp.exp(diff), 0.0)
                yd.append(_dot((cb_g * lm).astype(BF16), x_pb))
            y_diag = jnp.where(first, yd[0], yd[1])
            st_old = st_scr[j]
            y_off = _dot(c_g, st_old.astype(BF16)) * pair(e_cs, ha)
            ys.append(y_diag + y_off + d_ref[:, j * LANES:(j + 1) * LANES] * xs_p)
            xd = (x_p * pair(e_dec, ha)).astype(BF16)
            st_scr[j] = pair(e_tot, ha) * st_old + _dot(bt_g, xd)
    y = jnp.concatenate(ys, axis=1)

    zg = jnp.concatenate([z0[0], z1[0]], axis=1)
    y = y * _silu(zg)
    half = D_SSD // 2
    outs = []
    for g in range(2):
        yg = y[:, g * half:(g + 1) * half]
        ms = jnp.mean(yg * yg, axis=-1, keepdims=True)
        outs.append(yg * lax.rsqrt(ms + RMS_EPS) * ng_ref[:, g * half:(g + 1) * half])
    y_ref[0] = jnp.concatenate(outs, axis=1).astype(BF16)

    @pl.when(c == nchunks - 1)
    def _():
        for j in range(npair):
            st_ref[0, j * L:(j + 1) * L, :] = st_scr[j].T


def _ssd(u, wl, valid, prev=None, h0=None):
    b, t, _ = u.shape
    L = SSD_CHUNK
    nch = t // L
    has_init = prev is not None

    def ublk(k):
        return pl.BlockSpec((1, L, BLK), lambda i, c: (i, c, k))

    def full(a):
        return pl.BlockSpec(a.shape, lambda i, c: (0,) * a.ndim)

    consts = [wl['ssd_conv_w'], wl['ssd_conv_b'], wl['dtb_row'], wl['dtb_col'], wl['a_row'], wl['a_col'],
              wl['d_exp'], wl['ssd_norm']]
    in_specs = [ublk(BLK_Z), ublk(BLK_Z + 1), ublk(BLK_X), ublk(BLK_X + 1), ublk(BLK_X + 2),
                pl.BlockSpec((1, L, LANES), lambda i, c: (i, c, BLK_LAST * (BLK // LANES) + 1))]
    in_specs += [full(a) for a in consts]
    args = [u, u, u, u, u, u] + consts
    if has_init:
        in_specs += [pl.BlockSpec((1, 8, 3 * BLK), lambda i, c: (i, 0, 0)),
                     pl.BlockSpec((1, SSD_NHEADS * HEAD_DIM, SSD_STATE), lambda i, c: (i, 0, 0))]
        args += [prev, h0]
    return pl.pallas_call(
        functools.partial(_ssd_kernel, valid=valid, has_init=has_init, nchunks=nch),
        grid=(b, nch), in_specs=in_specs,
        out_specs=[pl.BlockSpec((1, L, D_SSD), lambda i, c: (i, c, 0)),
                   pl.BlockSpec((1, SSD_NHEADS * HEAD_DIM, SSD_STATE), lambda i, c: (i, 0, 0))],
        out_shape=[jax.ShapeDtypeStruct((b, t, D_SSD), BF16),
                   jax.ShapeDtypeStruct((b, SSD_NHEADS * HEAD_DIM, SSD_STATE), F32)],
        scratch_shapes=[pltpu.VMEM((L + 8, 3 * BLK), F32),
                        pltpu.VMEM((SSD_NHEADS // 2, SSD_STATE, LANES), F32)],
        compiler_params=_cp("arbitrary", "arbitrary"), name="ssd",
    )(*args)


def _prep_kernel(*refs, do_cum):
    (fq, fk, fv, dq, dk, dv, last, gm, gfq, gfk, gdq, gdk, fb) = refs[:13]
    if do_cum:
        (fqn, fkn, fkb, fvb, dqn, dkn, dkb, dvb, kid, logf_o, cumc, cumr, fvt, dvt, carry) = refs[13:]
    else:
        (fqn, fkn, fkb, fvb, dqn, dkn, dkb, dvb, kid, logf_o) = refs[13:]
    m = pl.program_id(1)
    gmat = gm[...]

    def headnorm(x, g_ref):
        sq = x * x
        hi = sq.astype(BF16)
        lo = (sq - hi.astype(F32)).astype(BF16)
        ms = _dot(hi, gmat) + _dot(lo, gmat)
        return x * lax.rsqrt(ms + RMS_EPS) * g_ref[...]

    fqn[0] = headnorm(fq[0], gfq).astype(BF16)
    k1 = headnorm(fk[0], gfk)
    fkn[0] = k1
    fkb[0] = k1.astype(BF16)
    fvb[0] = fv[0].astype(BF16)
    dqn[0] = headnorm(dq[0], gdq).astype(BF16)
    k2 = headnorm(dk[0], gdk)
    dkn[0] = k2
    dkb[0] = k2.astype(BF16)
    dvb[0] = dv[0].astype(BF16)
    lst = last[0]
    kid[0] = lst[:, 0:LANES].astype(BF16)
    small = lst[:, LANES:2 * LANES]
    logf = -_softplus(-(small + fb[...]))
    logf_o[0] = logf
    if do_cum:
        tm = small.shape[0]

        @pl.when(m == 0)
        def _():
            carry[...] = jnp.zeros_like(carry)

        lane = _iota(small.shape, 1)
        lf = jnp.where((lane >= SM_FF) & (lane < SM_FF + N_HEADS), logf, 0.0)
        cum = _dot3_l(_tri(tm, lambda r, c: r >= c), lf) + carry[0:1, :]
        carry[...] = jnp.broadcast_to(cum[tm - 1:tm, :], carry.shape)
        cumc[0] = cum
        cumr[0, 0] = cum.T[SM_FF:SM_FF + N_HEADS, :]
        for v_in, vt_out in ((fv, fvt), (dv, dvt)):
            vv = v_in[0]
            for j in range(D_ATT // LANES):
                vt_out[0, j, 0] = vv[:, j * LANES:(j + 1) * LANES].T.astype(BF16)


def _prep(u, wl, tm, do_cum):
    g, t, _ = u.shape
    tm = min(tm, t)
    nt = t // tm

    def ublk(k):
        return pl.BlockSpec((1, tm, BLK), lambda i, m: (i, m, k))

    def full(a):
        return pl.BlockSpec(a.shape, lambda i, m: (0,) * a.ndim)

    consts = [wl['gmat'], wl['fox_q_norm'], wl['fox_k_norm'], wl['dsa_q_norm'], wl['dsa_k_norm'], wl['fb_row']]
    in_specs = [ublk(BLK_FQ), ublk(BLK_FK), ublk(BLK_FV), ublk(BLK_DQ), ublk(BLK_DK), ublk(BLK_DV),
                ublk(BLK_LAST)] + [full(a) for a in consts]
    o512 = pl.BlockSpec((1, tm, BLK), lambda i, m: (i, m, 0))
    o128 = pl.BlockSpec((1, tm, LANES), lambda i, m: (i, m, 0))

    def s512(dt):
        return jax.ShapeDtypeStruct((g, t, BLK), dt)

    out_specs = [o512] * 8 + [o128, o128]
    out_shape = [s512(BF16), s512(F32), s512(BF16), s512(BF16), s512(BF16), s512(F32), s512(BF16), s512(BF16),
                 jax.ShapeDtypeStruct((g, t, LANES), BF16), jax.ShapeDtypeStruct((g, t, LANES), F32)]
    scratch = []
    if do_cum:
        vt_spec = pl.BlockSpec((1, D_ATT // LANES, 1, LANES, tm), lambda i, m: (i, 0, m, 0, 0))
        vt_shape = jax.ShapeDtypeStruct((g, D_ATT // LANES, nt, LANES, tm), BF16)
        out_specs += [o128, pl.BlockSpec((1, 1, N_HEADS, tm), lambda i, m: (i, m, 0, 0)), vt_spec, vt_spec]
        out_shape += [jax.ShapeDtypeStruct((g, t, LANES), F32),
                      jax.ShapeDtypeStruct((g, nt, N_HEADS, tm), F32), vt_shape, vt_shape]
        scratch = [pltpu.VMEM((8, LANES), F32)]
    return pl.pallas_call(
        functools.partial(_prep_kernel, do_cum=do_cum), grid=(g, nt),
        in_specs=in_specs, out_specs=out_specs, out_shape=out_shape, scratch_shapes=scratch,
        compiler_params=_cp("arbitrary", "arbitrary"), name="qk_prep",
    )(*([u] * 7 + consts))


_KEY_NEG_INF = -(2 ** 31) + 0x7FFFFF


def _key_to_float(key):
    bits = jnp.where(key < 0, key ^ jnp.int32(0x7FFFFFFF), key)
    return jnp.where(key <= _KEY_NEG_INF, -jnp.inf, lax.bitcast_convert_type(bits, F32))


def _kth_largest(count_ge, shape, k):
    imin = jnp.int32(-2 ** 31)

    def body(i, t):
        cand = t + lax.shift_left(jnp.int32(1), jnp.int32(31) - i)
        return jnp.where(count_ge(_key_to_float(cand)) >= k, cand, t)

    t = lax.fori_loop(0, 32, body, jnp.full(shape, imin, I32))
    return _key_to_float(t), _key_to_float(t + 1)


def _pairs_t(q_ref, qt_scr):
    upper = _iota((LANES, 1), 0) < HEAD_DIM
    for j in range(D_ATT // LANES):
        t = (q_ref[0, :, j * LANES:(j + 1) * LANES].astype(F32) * (HEAD_DIM ** -0.5)).T
        qt_scr[j] = jnp.concatenate([jnp.where(upper, t, 0.0), jnp.where(upper, 0.0, t)], axis=1).astype(BF16)


def _flash_t(qt_scr, k_ref, vt_ref, bias_of, nkb, tq, o_ref):
    upper = _iota((LANES, 1), 0) < HEAD_DIM

    def pick(x):
        return jnp.where(upper, x[:, 0:tq], x[:, tq:2 * tq])

    for j in range(D_ATT // LANES):
        sl = slice(j * LANES, (j + 1) * LANES)

        def body(i, carry, j=j, sl=sl):
            m_old, l_old, acc = carry
            kbs = (2 * i, jnp.minimum(2 * i + 1, nkb - 1))
            live = (True, 2 * i + 1 < nkb)
            ss = []
            for kb, ok in zip(kbs, live):
                ks = k_ref[0, pl.ds(pl.multiple_of(kb * tq, tq), tq), sl]
                s = _dot(ks, qt_scr[j]) + bias_of(j, kb)
                ss.append(s if ok is True else jnp.where(ok, s, NEG))
            m_new = jnp.maximum(m_old, jnp.maximum(jnp.max(ss[0], axis=0, keepdims=True),
                                                   jnp.max(ss[1], axis=0, keepdims=True)))
            alpha = jnp.exp(m_old - m_new)
            l_new = alpha * l_old
            o = None
            for kb, s in zip(kbs, ss):
                p = jnp.exp(s - m_new)
                l_new = l_new + jnp.sum(p, axis=0, keepdims=True)
                ov = _dot(vt_ref[0, j, kb], p.astype(BF16))
                o = ov if o is None else o + ov
            return m_new, l_new, pick(alpha) * acc + pick(o)

        init = (jnp.full((1, 2 * tq), NEG, F32), jnp.zeros((1, 2 * tq), F32), jnp.zeros((LANES, tq), F32))
        _, l, acc = lax.fori_loop(0, (nkb + 1) // 2, body, init)
        o_ref[0, :, sl] = (acc * pick(1.0 / l)).T.astype(BF16)


def _fox_prompt_kernel(q_ref, k_ref, vt_ref, cc_ref, cr_ref, o_ref, qt_scr, *, tq):
    qi = pl.program_id(1)
    _pairs_t(q_ref, qt_scr)
    kq = _iota((tq, 2 * tq), 0) - (_iota((tq, 2 * tq), 1) & (tq - 1))
    cq_rows = cr_ref[0, qi]

    def bias_of(j, kb):
        cc = cc_ref[0, pl.ds(pl.multiple_of(kb * tq, tq), tq), :]
        ck = jnp.concatenate([jnp.broadcast_to(cc[:, SM_FF + 2 * j + w:SM_FF + 2 * j + w + 1], (tq, tq))
                              for w in range(2)], axis=1)
        cq = jnp.concatenate([cq_rows[2 * j:2 * j + 1, :], cq_rows[2 * j + 1:2 * j + 2, :]], axis=1)
        return jnp.where(kq <= (qi - kb) * tq, cq - ck, NEG)

    _flash_t(qt_scr, k_ref, vt_ref, bias_of, qi + 1, tq, o_ref)


def _fox_prompt(q, k, vt, cumc, cumr, tq):
    b, t, _ = q.shape
    nt = t // tq
    assert tq & (tq - 1) == 0
    return pl.pallas_call(
        functools.partial(_fox_prompt_kernel, tq=tq), grid=(b, nt),
        in_specs=[pl.BlockSpec((1, tq, D_ATT), lambda i, m: (i, m, 0)),
                  pl.BlockSpec((1, t, D_ATT), lambda i, m: (i, 0, 0)),
                  pl.BlockSpec((1, D_ATT // LANES, nt, LANES, tq), lambda i, m: (i, 0, 0, 0, 0)),
                  pl.BlockSpec((1, t, LANES), lambda i, m: (i, 0, 0)),
                  pl.BlockSpec((1, nt, N_HEADS, tq), lambda i, m: (i, 0, 0, 0))],
        out_specs=pl.BlockSpec((1, tq, D_ATT), lambda i, m: (i, m, 0)),
        out_shape=jax.ShapeDtypeStruct((b, t, D_ATT), BF16),
        scratch_shapes=[pltpu.VMEM((D_ATT // LANES, LANES, 2 * tq), BF16)],
        compiler_params=_cp("arbitrary", "arbitrary"), name="fox_prompt",
    )(q, k, vt, cumc, cumr)


def _dsa_prompt_kernel(iq_ref, sm_ref, ki_ref, q_ref, k_ref, vt_ref, o_ref, sc_scr, sel_scr, qt_scr, *, tq, topk):
    qi = pl.program_id(1)
    nkb = qi + 1
    kq = _iota((tq, tq), 0) - _iota((tq, tq), 1)

    def causal(kb):
        return kq <= (qi - kb) * tq

    _pairs_t(iq_ref, qt_scr)
    w_t = (sm_ref[0] * (N_HEADS ** -0.5)).T

    def score_body(kb, _):
        ks = ki_ref[0, pl.ds(pl.multiple_of(kb * tq, tq), tq), :]
        acc = jnp.zeros((tq, tq), F32)
        for j in range(D_ATT // LANES):
            d = jnp.maximum(_dot(ks, qt_scr[j]), 0.0)
            for w in range(2):
                h = 2 * j + w
                acc = acc + w_t[SM_IW + h:SM_IW + h + 1, :] * d[:, w * tq:(w + 1) * tq]
        sc_scr[kb] = jnp.where(causal(kb), acc, -jnp.inf)
        return 0

    lax.fori_loop(0, nkb, score_body, 0)

    def count(fn):
        def body(kb, part):
            hit = jnp.where(fn(sc_scr[kb]), 1.0, 0.0)
            return part + jnp.sum(hit.reshape(tq // 8, 8, tq), axis=0)

        part = lax.fori_loop(0, nkb, body, jnp.zeros((8, tq), F32))
        return jnp.sum(part, axis=0, keepdims=True)

    kf = float(topk)
    thr, nxt = _kth_largest(lambda cand: count(lambda x: x >= cand), (1, tq), kf)
    need = kf - count(lambda x: x >= nxt)
    earlier = _tri(tq, lambda r, c: r > c)

    def sel_body(kb, run):
        blk = sc_scr[kb]
        tie = (blk >= thr) & (blk < nxt)
        tief = jnp.where(tie, 1.0, 0.0)
        before = _dot(earlier, tief.astype(BF16)) + run
        sel = ((blk >= nxt) | (tie & (before < need))) & causal(kb)
        bias = jnp.where(sel, 0.0, NEG)
        sel_scr[kb] = jnp.concatenate([bias, bias], axis=1)
        return run + jnp.sum(tief, axis=0, keepdims=True)

    lax.fori_loop(0, nkb, sel_body, jnp.zeros((1, tq), F32))

    _pairs_t(q_ref, qt_scr)
    _flash_t(qt_scr, k_ref, vt_ref, lambda j, kb: sel_scr[kb], nkb, tq, o_ref)


def _dsa_prompt(u, kid, q, k, vt, tq, topk):
    b, t, _ = q.shape
    nt = t // tq
    assert tq >= topk
    return pl.pallas_call(
        functools.partial(_dsa_prompt_kernel, tq=tq, topk=topk), grid=(b, nt),
        in_specs=[pl.BlockSpec((1, tq, BLK), lambda i, m: (i, m, BLK_IQ)),
                  pl.BlockSpec((1, tq, LANES), lambda i, m: (i, m, BLK_LAST * (BLK // LANES) + 1)),
                  pl.BlockSpec((1, t, LANES), lambda i, m: (i, 0, 0)),
                  pl.BlockSpec((1, tq, D_ATT), lambda i, m: (i, m, 0)),
                  pl.BlockSpec((1, t, D_ATT), lambda i, m: (i, 0, 0)),
                  pl.BlockSpec((1, D_ATT // LANES, nt, LANES, tq), lambda i, m: (i, 0, 0, 0, 0))],
        out_specs=pl.BlockSpec((1, tq, D_ATT), lambda i, m: (i, m, 0)),
        out_shape=jax.ShapeDtypeStruct((b, t, D_ATT), BF16),
        scratch_shapes=[pltpu.VMEM((nt, tq, tq), F32), pltpu.VMEM((nt, tq, 2 * tq), F32),
                        pltpu.VMEM((D_ATT // LANES, LANES, 2 * tq), BF16)],
        compiler_params=_cp("arbitrary", "arbitrary"), name="dsa_prompt",
    )(u, u, kid, q, k, vt)


def _fox_bias_kernel(pt_ref, nlf_ref, *refs, pg):
    lf_refs = refs[:pg]
    col_o, r_o, run_scr = refs[pg:]
    s = pl.program_id(1)
    strict = _tri(LANES, lambda r, c: r > c)

    @pl.when(s == 0)
    def _():
        nlf = nlf_ref[0]
        r_o[0] = _dot3_r(nlf, strict)
        run_scr[...] = jnp.broadcast_to(jnp.sum(nlf, axis=1, keepdims=True), run_scr.shape)

    run = run_scr[...]
    for i in range(pg):
        lf = lf_refs[i][0, 0]
        col_o[0, pg - 1 - i] = -(_dot3_r(lf, strict) + run)
        run = run + jnp.sum(lf, axis=1, keepdims=True)
    run_scr[...] = run


def _fox_bias(pt, lfc_t, layer, nlf, pg):
    b, npg = pt.shape
    nst = npg // pg

    def page(i):
        return lambda bi, s, pt: (layer, pt[bi, npg - 1 - (s * pg + i)], 0, 0)

    return pl.pallas_call(
        functools.partial(_fox_bias_kernel, pg=pg),
        grid_spec=pltpu.PrefetchScalarGridSpec(
            num_scalar_prefetch=1, grid=(b, nst),
            in_specs=[pl.BlockSpec((1, N_HEADS, LANES), lambda bi, s, pt: (bi, 0, 0))]
            + [pl.BlockSpec((1, 1, N_HEADS, LANES), page(i)) for i in range(pg)],
            out_specs=[pl.BlockSpec((1, pg, N_HEADS, LANES), lambda bi, s, pt: (bi, nst - 1 - s, 0, 0)),
                       pl.BlockSpec((1, N_HEADS, LANES), lambda bi, s, pt: (bi, 0, 0))],
            scratch_shapes=[pltpu.VMEM((N_HEADS, LANES), F32)]),
        out_shape=[jax.ShapeDtypeStruct((b, npg, N_HEADS, LANES), F32),
                   jax.ShapeDtypeStruct((b, N_HEADS, LANES), F32)],
        compiler_params=_cp("arbitrary", "arbitrary"), name="fox_bias",
    )(pt, nlf, *([lfc_t] * pg))


def _index_scores(dots, w, t_new):
    d = jnp.maximum(dots, 0.0) * w
    return jnp.sum(d.reshape(t_new, N_HEADS, d.shape[1]), axis=1)


def _idx_score_kernel(pt_ref, iq_ref, w_ref, *refs, pg, t_new):
    kit_refs = refs[:pg]
    o_ref = refs[pg]
    iq = (iq_ref[0] * (HEAD_DIM ** -0.5)).astype(BF16)
    w = w_ref[0] * (N_HEADS ** -0.5)
    pad = jnp.full((8 - t_new, LANES), -jnp.inf, F32)
    for i in range(pg):
        sc = _index_scores(_dot(iq, kit_refs[i][0, 0].astype(BF16)), w, t_new)
        o_ref[0, i] = jnp.concatenate([sc, pad], axis=0)


def _idx_scores(pt, kic_t, layer, iq, w, t_new, pg):
    b, npg = pt.shape
    rows = t_new * N_HEADS

    def page(i):
        return lambda bi, s, pt: (layer, pt[bi, s * pg + i], 0, 0)

    return pl.pallas_call(
        functools.partial(_idx_score_kernel, pg=pg, t_new=t_new),
        grid_spec=pltpu.PrefetchScalarGridSpec(
            num_scalar_prefetch=1, grid=(b, npg // pg),
            in_specs=[pl.BlockSpec((1, rows, HEAD_DIM), lambda bi, s, pt: (bi, 0, 0)),
                      pl.BlockSpec((1, rows, 1), lambda bi, s, pt: (bi, 0, 0))]
            + [pl.BlockSpec((1, 1, HEAD_DIM, LANES), page(i)) for i in range(pg)],
            out_specs=pl.BlockSpec((1, pg, 8, LANES), lambda bi, s, pt: (bi, s, 0, 0))),
        out_shape=jax.ShapeDtypeStruct((b, npg, 8, LANES), F32),
        compiler_params=_cp("arbitrary", "arbitrary"), name="dsa_idx_scores",
    )(pt, iq, w, *([kic_t] * pg))


def _select_kernel(sc_ref, iq_ref, w_ref, kin_ref, mf_o, mn_o, sc_scr, cnt_scr, *, t_new, topk):
    npg = sc_ref.shape[1]
    n1 = npg + 1
    sc_scr[0:npg] = sc_ref[0]
    iq = (iq_ref[0] * (HEAD_DIM ** -0.5)).astype(BF16)
    w = w_ref[0] * (N_HEADS ** -0.5)
    scn = _index_scores(_dot_nt(iq, kin_ref[0].astype(BF16)), w, t_new)
    scn = jnp.where(_iota(scn.shape, 1) <= _iota(scn.shape, 0), scn, -jnp.inf)
    sc_scr[npg] = jnp.concatenate([scn, jnp.full((8 - t_new, LANES), -jnp.inf, F32)], axis=0)
    sc = sc_scr[...]

    def count(pred):
        part = jnp.sum(jnp.where(pred, 1.0, 0.0), axis=0)
        return jnp.sum(part, axis=1, keepdims=True)

    kf = float(topk)
    thr, nxt = _kth_largest(lambda cand: count(sc >= cand), (8, 1), kf)
    need = kf - count(sc >= nxt)
    tie = (sc >= thr) & (sc < nxt)
    tief = jnp.where(tie, 1.0, 0.0)
    strict = _tri(LANES, lambda r, c: r < c)
    before = _dot(tief.reshape(n1 * 8, LANES).astype(BF16), strict).reshape(n1, 8, LANES)
    cnt_scr[...] = jnp.sum(tief, axis=2, keepdims=True)

    def body(p, run):
        c = cnt_scr[p]
        cnt_scr[p] = run
        return run + c

    lax.fori_loop(0, n1, body, jnp.zeros((8, 1), F32))
    sel = ((sc >= nxt) | (tie & (before + cnt_scr[...] < need))) & (sc > -jnp.inf)
    memb = jnp.where(sel, 1.0, 0.0)
    mf_o[0] = memb[0:npg]
    mn_o[0] = memb[npg]


def _dsa_select(scores, iq, w, ki_new, t_new, topk):
    b, npg, _, _ = scores.shape
    rows = t_new * N_HEADS
    return pl.pallas_call(
        functools.partial(_select_kernel, t_new=t_new, topk=topk), grid=(b,),
        in_specs=[pl.BlockSpec((1, npg, 8, LANES), lambda i: (i, 0, 0, 0)),
                  pl.BlockSpec((1, rows, HEAD_DIM), lambda i: (i, 0, 0)),
                  pl.BlockSpec((1, rows, 1), lambda i: (i, 0, 0)),
                  pl.BlockSpec((1, LANES, HEAD_DIM), lambda i: (i, 0, 0))],
        out_specs=[pl.BlockSpec((1, npg, 8, LANES), lambda i: (i, 0, 0, 0)),
                   pl.BlockSpec((1, 8, LANES), lambda i: (i, 0, 0))],
        out_shape=[jax.ShapeDtypeStruct((b, npg, 8, LANES), F32),
                   jax.ShapeDtypeStruct((b, 8, LANES), F32)],
        scratch_shapes=[pltpu.VMEM((npg + 1, 8, LANES), F32), pltpu.VMEM((npg + 1, 8, 1), F32)],
        compiler_params=_cp("arbitrary"), name="dsa_select",
    )(scores, iq, w, ki_new)


def _decode_kernel(pt_ref, q_ref, kn_ref, vn_ref, *refs, pg, nsteps, t_new, fox):
    kt_refs = refs[:pg]
    vt_refs = refs[pg:2 * pg]
    if fox:
        row_ref, col_ref, coln_ref, o_ref, qbd_scr, m_scr, l_scr, acc_scr = refs[2 * pg:]
    else:
        msk_ref, mskn_ref, o_ref, qbd_scr, m_scr, l_scr, acc_scr = refs[2 * pg:]
    s = pl.program_id(1)
    rows = t_new * N_HEADS
    lane = _iota((N_HEADS, D_ATT), 1)
    hrow = _iota((N_HEADS, D_ATT), 0)
    hmask = (lane >= hrow * HEAD_DIM) & (lane < (hrow + 1) * HEAD_DIM)

    def per_query(me):
        return jnp.concatenate([jnp.broadcast_to(me[t:t + 1, :], (N_HEADS, LANES)) for t in range(t_new)], axis=0)

    def per_head(me):
        return jnp.concatenate([me] * t_new, axis=0)

    @pl.when(s == 0)
    def _():
        q = q_ref[0].astype(F32) * (HEAD_DIM ** -0.5)
        for t in range(t_new):
            qt = jnp.broadcast_to(q[t:t + 1, :], (N_HEADS, D_ATT))
            qbd_scr[t * N_HEADS:(t + 1) * N_HEADS, :] = jnp.where(hmask, qt, 0.0).astype(BF16)
        sn = _dot_nt(qbd_scr[...], kn_ref[0])
        if fox:
            causal = _iota((rows, LANES), 1) <= lax.shift_right_logical(_iota((rows, LANES), 0), 3)
            sn = sn + jnp.where(causal, row_ref[0] - per_head(coln_ref[0]), NEG)
        else:
            sn = sn + jnp.where(per_query(mskn_ref[0]) > 0.5, 0.0, NEG)
        m = jnp.max(sn, axis=1, keepdims=True)
        p = jnp.exp(sn - m)
        m_scr[...] = m
        l_scr[...] = jnp.sum(p, axis=1, keepdims=True)
        acc_scr[...] = _dot(p.astype(BF16), vn_ref[0])

    qbd = qbd_scr[...]
    ss = []
    for i in range(pg):
        if fox:
            bias = row_ref[0] - per_head(col_ref[0, i])
        else:
            bias = jnp.where(per_query(msk_ref[0, i]) > 0.5, 0.0, NEG)
        ss.append(_dot(qbd, kt_refs[i][0, 0].astype(BF16)) + bias)
    m_old = m_scr[...]
    m_new = m_old
    for x in ss:
        m_new = jnp.maximum(m_new, jnp.max(x, axis=1, keepdims=True))
    alpha = jnp.exp(m_old - m_new)
    l = alpha * l_scr[...]
    acc = alpha * acc_scr[...]
    for i, x in enumerate(ss):
        p = jnp.exp(x - m_new)
        l = l + jnp.sum(p, axis=1, keepdims=True)
        acc = acc + _dot_nt(p.astype(BF16), vt_refs[i][0, 0].astype(BF16))
    m_scr[...] = m_new
    l_scr[...] = l
    acc_scr[...] = acc

    @pl.when(s == nsteps - 1)
    def _():
        out = acc / l
        ys = []
        for t in range(t_new):
            blk = jnp.where(hmask, out[t * N_HEADS:(t + 1) * N_HEADS, :], 0.0)
            ys.append(jnp.sum(blk, axis=0, keepdims=True))
        o_ref[0] = jnp.concatenate(ys, axis=0).astype(BF16)


def _decode_attn(pt, q, kc_t, vc_t, layer, k_new, v_new, bias_args, t_new, pg, fox):
    b, npg = pt.shape
    rows = t_new * N_HEADS
    nst = npg // pg

    def page(i):
        return lambda bi, s, pt: (layer, pt[bi, s * pg + i], 0, 0)

    def per_b(shape):
        return pl.BlockSpec((1,) + shape, lambda bi, s, pt: (bi,) + (0,) * len(shape))

    cache_specs = [pl.BlockSpec((1, 1, D_ATT, LANES), page(i)) for i in range(pg)]
    paged = pl.BlockSpec((1, pg, 8, LANES), lambda bi, s, pt: (bi, s, 0, 0))
    bias_specs = ([per_b((rows, 1)), paged, per_b((8, LANES))] if fox else [paged, per_b((8, LANES))])
    return pl.pallas_call(
        functools.partial(_decode_kernel, pg=pg, nsteps=nst, t_new=t_new, fox=fox),
        grid_spec=pltpu.PrefetchScalarGridSpec(
            num_scalar_prefetch=1, grid=(b, nst),
            in_specs=[per_b((t_new, D_ATT)), per_b((LANES, D_ATT)), per_b((LANES, D_ATT))]
            + cache_specs + cache_specs + bias_specs,
            out_specs=per_b((t_new, D_ATT)),
            scratch_shapes=[pltpu.VMEM((rows, D_ATT), BF16), pltpu.VMEM((rows, 1), F32),
                            pltpu.VMEM((rows, 1), F32), pltpu.VMEM((rows, D_ATT), F32)]),
        out_shape=jax.ShapeDtypeStruct((b, t_new, D_ATT), BF16),
        compiler_params=_cp("arbitrary", "arbitrary"), name="decode_attn",
    )(pt, q, k_new, v_new, *([kc_t] * pg), *([vc_t] * pg), *bias_args)


def _pack_w_in(w_in):
    o = 0
    seg = {}
    for name, n in (('z', 1024), ('xbc', 1536), ('dt', 16), ('fq', 512), ('fk', 512), ('fv', 512), ('ff', 8),
                    ('dq', 512), ('dk', 512), ('dv', 512), ('iq', 512), ('ik', 64), ('iw', 8)):
        seg[name] = w_in[:, :, o:o + n]
        o += n
    used = 1024 + 1536 + 7 * 512 + 64 + 64 + 16 + 8 + 8
    pad = jnp.zeros(w_in.shape[:2] + (N_PACK - used,), w_in.dtype)
    parts = [seg[k] for k in ('z', 'xbc', 'fq', 'fk', 'fv', 'dq', 'dk', 'dv', 'iq', 'ik', 'ik', 'dt', 'ff', 'iw')]
    return jnp.concatenate(parts + [pad], axis=-1).astype(BF16)


def _lane_row(vals, offset):
    return jnp.zeros((1, LANES), F32).at[0, offset:offset + vals.shape[0]].set(vals.astype(F32))


def _layer_consts(l, P):
    a = -jnp.exp(P['ssd_a_log'][l].astype(F32))
    dtb_row = _lane_row(P['ssd_dt_bias'][l], SM_DT)
    a_row = _lane_row(a, SM_DT)
    gmat = jnp.kron(jnp.eye(N_HEADS, dtype=F32), jnp.full((HEAD_DIM, HEAD_DIM), 1.0 / HEAD_DIM, F32)).astype(BF16)

    def tile_heads(g):
        return jnp.tile(g.astype(F32), N_HEADS).reshape(1, D_ATT)

    return dict(
        ssd_conv_w=P['ssd_conv_w'][l], ssd_conv_b=P['ssd_conv_b'][l].reshape(1, -1),
        dtb_row=dtb_row, dtb_col=dtb_row.reshape(LANES, 1), a_row=a_row, a_col=a_row.reshape(LANES, 1),
        d_exp=jnp.repeat(P['ssd_d'][l].astype(F32), HEAD_DIM).reshape(1, D_SSD),
        ssd_norm=P['ssd_norm'][l].reshape(1, D_SSD), gmat=gmat,
        fox_q_norm=tile_heads(P['fox_q_norm'][l]), fox_k_norm=tile_heads(P['fox_k_norm'][l]),
        dsa_q_norm=tile_heads(P['dsa_q_norm'][l]), dsa_k_norm=tile_heads(P['dsa_k_norm'][l]),
        fb_row=_lane_row(P['fox_f_bias'][l], SM_FF),
    )


def _pad_rows(x, rows):
    return jnp.pad(x, ((0, 0), (0, rows - x.shape[1]), (0, 0)))


def _layer(l, x, mod, P, wl, past):
    b, t, d = x.shape
    sh1, sc1, g1, sh2, sc2, g2 = [mod[:, None, i * d:(i + 1) * d] for i in range(6)]
    prompt = past is None
    m = b * t
    f2 = P['w_up'].shape[-1]

    h = _norm_mod(x, P['attn_norm'][l], sc1, sh1)
    if prompt:
        u = _matmul(h, P['w_in_p'][l], 512, TN_IN)
        uf = u
    else:
        uf = _matmul(h.reshape(1, m, d), P['w_in_p'][l], m, TN_IN)
        u = uf.reshape(b, t, N_PACK)

    if prompt:
        y_ssd, st = _ssd(u, wl, SSD_CHUNK)
    else:
        prev = jnp.pad(past['state_ssd_conv'][l], ((0, 0), (8 - (SSD_CONV - 1), 0), (0, 0)))
        h0 = past['state_ssd'][l].reshape(b, SSD_NHEADS * HEAD_DIM, SSD_STATE)
        y_ssd, st = _ssd(_pad_rows(u, SSD_CHUNK), wl, t, prev, h0)
        y_ssd = y_ssd[:, :t]
    ssd_new = st.reshape(b, SSD_NHEADS, HEAD_DIM, SSD_STATE)
    xbc_raw = u[:, :, BLK_Z * BLK + D_SSD:BLK_Z * BLK + D_SSD + 3 * BLK]
    if prompt:
        ssd_conv_new = xbc_raw[:, t - (SSD_CONV - 1):]
    else:
        ssd_conv_new = jnp.concatenate([past['state_ssd_conv'][l], xbc_raw], axis=1)[:, t:]

    tq = 256
    (fqn, fkn, fkb, fvb, dqn, dkn, dkb, dvb, kid, logf, *cum) = _prep(uf, wl, tq, prompt)
    fk_out = fkn.reshape(b, t, N_HEADS, HEAD_DIM)
    dk_out = dkn.reshape(b, t, N_HEADS, HEAD_DIM)
    fv_out = u[:, :, BLK_FV * BLK:(BLK_FV + 1) * BLK].reshape(b, t, N_HEADS, HEAD_DIM)
    dv_out = u[:, :, BLK_DV * BLK:(BLK_DV + 1) * BLK].reshape(b, t, N_HEADS, HEAD_DIM)
    ik_out = u[:, :, BLK_LAST * BLK:BLK_LAST * BLK + HEAD_DIM]
    logf_out = logf.reshape(b, t, LANES)[:, :, SM_FF:SM_FF + N_HEADS]

    if prompt:
        cumc, cumr, fvt, dvt = cum
        y_fox = _fox_prompt(fqn, fkb, fvt, cumc, cumr, tq)
        y_dsa = _dsa_prompt(u, kid, dqn, dkb, dvt, tq, min(TOPK_MAX, t // 4))
    else:
        pt = past['page_table']
        npg = pt.shape[1]
        page = past['kidx_t'].shape[3]
        assert page == LANES
        rows = t * N_HEADS
        pg_kv = min(PAGES_PER_STEP_KV, npg)
        pg_aux = min(PAGES_PER_STEP_AUX, npg)

        def r3(a):
            return a.reshape(b, t, -1)

        def new_page(a):
            return _pad_rows(r3(a), LANES)

        nlf = jnp.pad(jnp.swapaxes(logf_out, 1, 2), ((0, 0), (0, 0), (0, LANES - t)))
        col, rnew = _fox_bias(pt, past['fox_logf_t'], l, nlf, pg_aux)
        row_term = jnp.swapaxes(-rnew[:, :, :t], 1, 2).reshape(b, rows, 1)
        y_fox = _decode_attn(pt, r3(fqn), past['fox_kt'], past['fox_vt'], l, new_page(fkb), new_page(fvb),
                             (row_term, col, -rnew), t, pg_kv, True)

        iq = u[:, :, BLK_IQ * BLK:(BLK_IQ + 1) * BLK].reshape(b, rows, HEAD_DIM)
        iw = u[:, :, BLK_LAST * BLK + LANES + SM_IW:BLK_LAST * BLK + LANES + SM_IW + N_HEADS].reshape(b, rows, 1)
        sc = _idx_scores(pt, past['kidx_t'], l, iq, iw, t, pg_aux)
        memb, memb_new = _dsa_select(sc, iq, iw, _pad_rows(ik_out, LANES), t,
                                     min(TOPK_MAX, (npg * page + t) // 4))
        y_dsa = _decode_attn(pt, r3(dqn), past['dsa_kt'], past['dsa_vt'], l, new_page(dkb), new_page(dvb),
                             (memb, memb_new), t, pg_kv, False)

    mixed = (y_ssd, y_fox.reshape(b, t, D_ATT), y_dsa.reshape(b, t, D_ATT))
    if prompt:
        x = _out_proj(mixed, P['w_out_b'][l], 512, x, g1)
    else:
        x = _out_proj([a.reshape(1, m, -1) for a in mixed], P['w_out_b'][l], m, x.reshape(1, m, d),
                      jnp.broadcast_to(g1, (b, t, d)).reshape(1, m, d)).reshape(b, t, d)

    h2 = _norm_mod(x, P['ffn_norm'][l], sc2, sh2)
    cw = P['ffn_conv_w'][l]
    cb = P['ffn_conv_b'][l].reshape(1, f2)
    if prompt:
        act, ta, tg = _ffn_up_prompt(h2.reshape(m, d), P['w_up_b'][l], cw, cb, t)
        ffn_conv_new = jnp.concatenate([ta, tg], axis=-1)[:, 8 - (FFN_CONV - 1):]
        x = _matmul(act.reshape(b, t, -1), P['w_down_b'][l], 512, TN_DOWN, res=x, gate=g2)
    else:
        buf = past['state_ffn_conv'][l]
        zero = jnp.zeros((b, t - 1, f2), F32)
        p1 = jnp.concatenate([buf[:, 1:2], zero], axis=1).reshape(m, f2)
        p2 = jnp.concatenate([buf[:, 0:2], zero[:, 1:]], axis=1).reshape(m, f2)
        act, ra, rg = _ffn_up_sample(h2.reshape(m, d), P['w_up_b'][l], cw, cb, p1, p2, t)
        raw = jnp.concatenate([ra, rg], axis=-1).reshape(b, t, f2)
        ffn_conv_new = jnp.concatenate([buf, raw], axis=1)[:, t:]
        x = _matmul(act.reshape(1, m, -1), P['w_down_b'][l], m, TN_DOWN, res=x.reshape(1, m, d),
                    gate=jnp.broadcast_to(g2, (b, t, d)).reshape(1, m, d)).reshape(b, t, d)

    new = (fk_out, fv_out, logf_out, dk_out, dv_out, ik_out, ssd_new, ssd_conv_new, ffn_conv_new)
    return x, new


def kernel(x_prompt, x_sample, cache_fox_k, cache_fox_v, cache_fox_logf, cache_dsa_k, cache_dsa_v, cache_dsa_kidx, state_ssd, state_ssd_conv, state_ffn_conv, page_table, c_prompt, c_sample, w_ada, b_ada, attn_norm, w_in, ssd_conv_w, ssd_conv_b, ssd_dt_bias, ssd_a_log, ssd_d, ssd_norm, fox_q_norm, fox_k_norm, fox_f_bias, dsa_q_norm, dsa_k_norm, w_out, ffn_norm, w_up, ffn_conv_w, ffn_conv_b, w_down):
    depth = w_in.shape[0]
    bp = x_prompt.shape[0]
    bs = x_sample.shape[0]
    P = dict(attn_norm=attn_norm, ssd_conv_w=ssd_conv_w, ssd_conv_b=ssd_conv_b, ssd_dt_bias=ssd_dt_bias,
             ssd_a_log=ssd_a_log, ssd_d=ssd_d, ssd_norm=ssd_norm, fox_q_norm=fox_q_norm, fox_k_norm=fox_k_norm,
             fox_f_bias=fox_f_bias, dsa_q_norm=dsa_q_norm, dsa_k_norm=dsa_k_norm, ffn_norm=ffn_norm,
             ffn_conv_w=ffn_conv_w, ffn_conv_b=ffn_conv_b, w_up=w_up,
             w_in_p=_pack_w_in(w_in), w_out_b=w_out.astype(BF16), w_up_b=w_up.astype(BF16),
             w_down_b=w_down.astype(BF16))
    npool, page = cache_fox_k.shape[1], cache_fox_k.shape[2]

    def positions_minor(c):
        return jnp.transpose(c, (0, 1, 3, 4, 2)).reshape(depth, npool, N_HEADS * HEAD_DIM, page)

    past = dict(state_ssd=state_ssd, state_ssd_conv=state_ssd_conv, state_ffn_conv=state_ffn_conv,
                page_table=page_table,
                fox_kt=positions_minor(cache_fox_k), fox_vt=positions_minor(cache_fox_v),
                dsa_kt=positions_minor(cache_dsa_k), dsa_vt=positions_minor(cache_dsa_v),
                kidx_t=jnp.swapaxes(cache_dsa_kidx, 2, 3),
                fox_logf_t=jnp.swapaxes(cache_fox_logf, 2, 3))

    rows = bp + bs
    rows_pad = -(-rows // 8) * 8
    c_all = jnp.pad(jnp.concatenate([c_prompt, c_sample], axis=0), ((0, rows_pad - rows), (0, 0)))
    mods = _ada_mod(c_all, w_ada, b_ada)

    xp, xs = x_prompt, x_sample
    new_p, new_s = [], []
    for l in range(depth):
        wl = _layer_consts(l, P)
        xp, sp = _layer(l, xp, mods[l, :bp], P, wl, None)
        xs, ss = _layer(l, xs, mods[l, bp:rows], P, wl, past)
        new_p.append(sp)
        new_s.append(ss)
    outs_p = [jnp.stack(a, 0) for a in zip(*new_p)]
    outs_s = [jnp.stack(a, 0) for a in zip(*new_s)]
    return tuple([xp, xs] + outs_p + outs_s)
```
